```python
import jax, jax.numpy as jnp
from jax import lax
import numpy as np

D_MODEL = 2048
BATCH = 2
SEQ = 16384
DEPTH = 1
DEC_BATCH = 8
DEC_SEQ = 64
PAST_LEN = 1024

CHUNK = 64
D_MIX = D_MODEL
D_ATTN = D_MIX // 2
N_HEADS_A = 16
HEAD_DIM = D_ATTN // N_HEADS_A
D_CONV = D_MIX - D_ATTN
CONV_WIDTH = 31
N_EXPERTS = 32
TOP_K = 4
D_FF = D_MODEL
SWIGLU_LIMIT = 7.0
SWIGLU_ALPHA = 1.702
Q_BLOCK = 128
EPS = 1e-5
NEG = -1e30
FORGET_BIAS_INIT = 3.0
D_IN = 3 * D_ATTN + N_HEADS_A + 2 * D_CONV
SPLITS = (D_ATTN, 2 * D_ATTN, 3 * D_ATTN, 3 * D_ATTN + N_HEADS_A, 3 * D_ATTN + N_HEADS_A + D_CONV)

kernel_name = 'fox_conformer_moe_stream_step'


def rmsnorm(x, g):
    xf = x.astype(jnp.float32)
    y = xf * lax.rsqrt(jnp.mean(xf * xf, axis=-1, keepdims=True) + EPS)
    return (y * g.astype(jnp.float32)).astype(x.dtype)


def layernorm(x, g, b):
    xf = x.astype(jnp.float32)
    mu = jnp.mean(xf, axis=-1, keepdims=True)
    var = jnp.mean(jnp.square(xf - mu), axis=-1, keepdims=True)
    y = (xf - mu) * lax.rsqrt(var + EPS)
    return (y * g.astype(jnp.float32) + b.astype(jnp.float32)).astype(x.dtype)


def fox_attention(q, k, v, Fq, Fk, q_pos, k_pos, block):
    B, Tq, H, Dh = q.shape
    nb = Tq // block
    qb = q.reshape(B, nb, block, H, Dh).transpose(1, 0, 2, 3, 4)
    Fqb = Fq.reshape(B, nb, block, H).transpose(1, 0, 2, 3)
    pb = q_pos.reshape(nb, block)
    FkT = Fk.transpose(0, 2, 1)[:, :, None, :]
    scale = HEAD_DIM ** -0.5

    def one_block(args):
        qi, Fi, pi = args
        s = jnp.einsum('bqhd,bkhd->bhqk', qi, k).astype(jnp.float32) * scale
        s = s + Fi.transpose(0, 2, 1)[..., None] - FkT
        mask = k_pos[None, :] <= pi[:, None]
        p = jax.nn.softmax(jnp.where(mask, s, NEG), axis=-1)
        return jnp.einsum('bhqk,bkhd->bqhd', p.astype(v.dtype), v)

    out = lax.map(one_block, (qb, Fqb, pb))
    return out.transpose(1, 0, 2, 3, 4).reshape(B, Tq, H * Dh)


def causal_dwconv(u_all, w_dw, b_dw):
    C = u_all.shape[-1]
    y = lax.conv_general_dilated(u_all, w_dw[:, None, :], window_strides=(1,), padding='VALID',
                                 dimension_numbers=('NWC', 'WIO', 'NWC'), feature_group_count=C)
    return y + b_dw


def token_mixer(xn, k_past, v_past, logf_past, conv_past, w_in, b_f, w_dw, b_dw, g_ln, b_ln, w_out, block):
    B, T, _ = xn.shape
    P = k_past.shape[1]
    z = xn @ w_in
    q, k, v, f_logit, a, g = jnp.split(z, SPLITS, axis=-1)
    q = q.reshape(B, T, N_HEADS_A, HEAD_DIM)
    k = k.reshape(B, T, N_HEADS_A, HEAD_DIM)
    v = v.reshape(B, T, N_HEADS_A, HEAD_DIM)
    logf = jax.nn.log_sigmoid(f_logit.astype(jnp.float32) + b_f.astype(jnp.float32))
    k_all = jnp.concatenate([k_past, k], axis=1)
    v_all = jnp.concatenate([v_past, v], axis=1)
    F_all = jnp.cumsum(jnp.concatenate([logf_past.astype(jnp.float32), logf], axis=1), axis=1)
    pos = jnp.arange(P + T, dtype=jnp.int32)
    attn = fox_attention(q, k_all, v_all, F_all[:, P:], F_all, pos[P:], pos, block)
    u = a * jax.nn.sigmoid(g)
    u_all = jnp.concatenate([conv_past, u], axis=1)
    c = jax.nn.silu(layernorm(causal_dwconv(u_all, w_dw, b_dw), g_ln, b_ln))
    out = jnp.concatenate([attn, c.astype(attn.dtype)], axis=-1) @ w_out
    new_conv = u_all[:, u_all.shape[1] - (CONV_WIDTH - 1):]
    return out, k, v, logf, new_conv


def moe(x, w_router, b_router, w_gate, b_gate, w_up, b_up, w_down, b_down):
    B, T, D = x.shape
    xt = x.reshape(B * T, D)
    logits = (xt @ w_router).astype(jnp.float32) + b_router.astype(jnp.float32)
    top_v, top_i = lax.top_k(logits, TOP_K)
    top_w = jax.nn.softmax(top_v, axis=-1)
    gates = jnp.sum(jax.nn.one_hot(top_i, N_EXPERTS, dtype=jnp.float32) * top_w[..., None], axis=1)
    out = jnp.zeros((B * T, D), jnp.float32)
    for e in range(N_EXPERTS):
        gt = jnp.minimum(xt @ w_gate[e] + b_gate[e], SWIGLU_LIMIT)
        up = jnp.clip(xt @ w_up[e] + b_up[e], -SWIGLU_LIMIT, SWIGLU_LIMIT)
        hid = (up + 1) * (gt * jax.nn.sigmoid(SWIGLU_ALPHA * gt))
        out = out + gates[:, e:e + 1] * (hid @ w_down[e] + b_down[e]).astype(jnp.float32)
    return out.astype(x.dtype).reshape(B, T, D)


def layer(x, k_past, v_past, logf_past, conv_past, block, g_norm_mix, w_in, b_f, w_dw, b_dw, g_ln, b_ln, w_out,
          g_norm_ffn, w_router, b_router, w_gate, b_gate, w_up, b_up, w_down, b_down):
    mix, k, v, logf, new_conv = token_mixer(rmsnorm(x, g_norm_mix), k_past, v_past, logf_past, conv_past,
                                            w_in, b_f, w_dw, b_dw, g_ln, b_ln, w_out, block)
    h = x + mix
    h = h + moe(rmsnorm(h, g_norm_ffn), w_router, b_router, w_gate, b_gate, w_up, b_up, w_down, b_down)
    return h, k, v, logf, new_conv


def setup_inputs(seed: int = 0) -> dict:
    key = jax.random.key(seed)
    ks = jax.random.split(key, 26)
    f32 = jnp.float32

    def nrm(k, shape, scale):
        return jax.random.normal(k, shape, f32) * scale

    return {
        'x_prompt': nrm(ks[0], (BATCH, SEQ, D_MODEL), 1.0),
        'x_sample': nrm(ks[1], (DEC_BATCH, DEC_SEQ, D_MODEL), 1.0),
        'cache_k': nrm(ks[2], (DEC_BATCH, PAST_LEN, N_HEADS_A, HEAD_DIM), 1.0),
        'cache_v': nrm(ks[3], (DEC_BATCH, PAST_LEN, N_HEADS_A, HEAD_DIM), 1.0),
        'cache_logf': jax.nn.log_sigmoid(FORGET_BIAS_INIT + nrm(ks[4], (DEC_BATCH, PAST_LEN, N_HEADS_A), 0.5)),
        'cache_conv': nrm(ks[5], (DEC_BATCH, CONV_WIDTH - 1, D_CONV), 0.5),
        'g_norm_mix': 1.0 + nrm(ks[6], (D_MODEL,), 0.02),
        'w_in': nrm(ks[7], (D_MODEL, D_IN), D_MODEL ** -0.5),
        'b_f': FORGET_BIAS_INIT + nrm(ks[8], (N_HEADS_A,), 0.1),
        'w_dw': nrm(ks[9], (CONV_WIDTH, D_CONV), CONV_WIDTH ** -0.5),
        'b_dw': nrm(ks[10], (D_CONV,), 0.01),
        'g_ln': 1.0 + nrm(ks[11], (D_CONV,), 0.02),
        'b_ln': nrm(ks[12], (D_CONV,), 0.01),
        'w_out': nrm(ks[13], (D_MIX, D_MODEL), D_MIX ** -0.5),
        'g_norm_ffn': 1.0 + nrm(ks[14], (D_MODEL,), 0.02),
        'w_router': nrm(ks[15], (D_MODEL, N_EXPERTS), D_MODEL ** -0.5),
        'b_router': nrm(ks[16], (N_EXPERTS,), 0.01),
        'w_gate': nrm(ks[17], (N_EXPERTS, D_MODEL, D_FF), D_MODEL ** -0.5),
        'b_gate': nrm(ks[18], (N_EXPERTS, D_FF), 0.01),
        'w_up': nrm(ks[19], (N_EXPERTS, D_MODEL, D_FF), D_MODEL ** -0.5),
        'b_up': nrm(ks[20], (N_EXPERTS, D_FF), 0.01),
        'w_down': nrm(ks[21], (N_EXPERTS, D_FF, D_MODEL), D_FF ** -0.5),
        'b_down': nrm(ks[22], (N_EXPERTS, D_MODEL), 0.01),
        'g_norm_final': 1.0 + nrm(ks[23], (D_MODEL,), 0.02),
    }


def reference(x_prompt, x_sample, cache_k, cache_v, cache_logf, cache_conv, g_norm_mix, w_in, b_f, w_dw, b_dw,
              g_ln, b_ln, w_out, g_norm_ffn, w_router, b_router, w_gate, b_gate, w_up, b_up, w_down, b_down,
              g_norm_final):
    assert x_sample.shape[1] <= CHUNK
    weights = (g_norm_mix, w_in, b_f, w_dw, b_dw, g_ln, b_ln, w_out,
               g_norm_ffn, w_router, b_router, w_gate, b_gate, w_up, b_up, w_down, b_down)
    B = x_prompt.shape[0]
    k0 = jnp.zeros((B, 0, N_HEADS_A, HEAD_DIM), x_prompt.dtype)
    logf0 = jnp.zeros((B, 0, N_HEADS_A), jnp.float32)
    conv0 = jnp.zeros((B, CONV_WIDTH - 1, D_CONV), x_prompt.dtype)
    h_p, k_prompt, v_prompt, logf_prompt, conv_prompt = layer(x_prompt, k0, k0, logf0, conv0, Q_BLOCK, *weights)
    y_prompt = rmsnorm(h_p, g_norm_final)
    h_s, k_sample, v_sample, logf_sample, conv_sample = layer(x_sample, cache_k, cache_v, cache_logf, cache_conv,
                                                              x_sample.shape[1], *weights)
    y_sample = rmsnorm(h_s, g_norm_final)
    return (y_prompt, y_sample, k_prompt, v_prompt, logf_prompt, conv_prompt,
            k_sample, v_sample, logf_sample, conv_sample)
```

```python
import functools

import jax
import jax.numpy as jnp
from jax import lax
from jax.experimental import pallas as pl
from jax.experimental.pallas import tpu as pltpu

F32 = jnp.float32
BF16 = jnp.bfloat16
I32 = jnp.int32

D_ATTN = 1024
N_HEADS = 16
HEAD_DIM = 64
D_CONV = 1024
CONV_WIDTH = 31
HIST = CONV_WIDTH - 1
N_EXPERTS = 32
TOP_K = 4
SWIGLU_LIMIT = 7.0
SWIGLU_ALPHA = 1.702
EPS = 1e-5
NEG = -1e30

LANES = 128
HIST_PAD = 32
VMEM_LIMIT = 56 * 1024 * 1024


def _cparams(n_axes, vmem=VMEM_LIMIT):
    return pltpu.CompilerParams(dimension_semantics=("arbitrary",) * n_axes, vmem_limit_bytes=vmem)


def _dot(a, b):
    return jnp.dot(a, b, preferred_element_type=F32)


def _const_spec(shape):
    nd = len(shape)
    return pl.BlockSpec(shape, lambda *_: (0,) * nd, pipeline_mode=pl.Buffered(1))


def _in_proj_kernel(x_ref, g_ref, wq_ref, wk_ref, wv_ref, wf_ref, wa_ref, wg_ref, bf_ref,
                    q_ref, k_ref, v_ref, kb_ref, vb_ref, logf_ref, u_ref):
    x = x_ref[...]
    ms = jnp.mean(x * x, axis=-1, keepdims=True)
    xn = (x * lax.rsqrt(ms + EPS) * g_ref[...]).astype(BF16)
    q_ref[...] = (_dot(xn, wq_ref[...]) * (HEAD_DIM ** -0.5)).astype(BF16)
    k = _dot(xn, wk_ref[...])
    k_ref[...] = k
    kb_ref[...] = k.astype(BF16)
    v = _dot(xn, wv_ref[...])
    v_ref[...] = v
    vb_ref[...] = v.astype(BF16)
    fl = _dot(xn, wf_ref[...]) + bf_ref[...]
    logf = jnp.minimum(fl, 0.0) - jnp.log1p(jnp.exp(-jnp.abs(fl)))
    logf_ref[...] = logf[:, :N_HEADS]
    a = _dot(xn, wa_ref[...])
    g = _dot(xn, wg_ref[...])
    u_ref[...] = a * jax.nn.sigmoid(g)


def _in_proj(x, g_mix, wq, wk, wv, wf, wa, wg, bf_pad, tm):
    n, d = x.shape
    row = lambda w: pl.BlockSpec((tm, w), lambda i: (i, 0))
    return pl.pallas_call(
        _in_proj_kernel,
        grid=(n // tm,),
        in_specs=[row(d), _const_spec((1, d)), _const_spec(wq.shape), _const_spec(wk.shape), _const_spec(wv.shape),
                  _const_spec(wf.shape), _const_spec(wa.shape), _const_spec(wg.shape), _const_spec((1, LANES))],
        out_specs=[row(D_ATTN), row(D_ATTN), row(D_ATTN), row(D_ATTN), row(D_ATTN), row(N_HEADS), row(D_CONV)],
        out_shape=[jax.ShapeDtypeStruct((n, D_ATTN), BF16), jax.ShapeDtypeStruct((n, D_ATTN), F32),
                   jax.ShapeDtypeStruct((n, D_ATTN), F32), jax.ShapeDtypeStruct((n, D_ATTN), BF16),
                   jax.ShapeDtypeStruct((n, D_ATTN), BF16), jax.ShapeDtypeStruct((n, N_HEADS), F32),
                   jax.ShapeDtypeStruct((n, D_CONV), F32)],
        compiler_params=_cparams(1),
        name="in_proj",
    )(x, g_mix, wq, wk, wv, wf, wa, wg, bf_pad)


def _cumsum_kernel(x_ref, o_ref, carry_ref):
    j = pl.program_id(1)

    @pl.when(j == 0)
    def _():
        carry_ref[...] = jnp.zeros_like(carry_ref)

    x = x_ref[...]
    tb = x.shape[1]
    x1 = x.astype(BF16)
    r1 = x - x1.astype(F32)
    x2 = r1.astype(BF16)
    x3 = (r1 - x2.astype(F32)).astype(BF16)
    tri = (lax.broadcasted_iota(I32, (tb, tb), 0) <= lax.broadcasted_iota(I32, (tb, tb), 1)).astype(BF16)
    c = _dot(x1, tri) + _dot(x2, tri) + _dot(x3, tri) + carry_ref[...]
    o_ref[...] = -c
    carry_ref[...] = c[:, tb - 1:tb]


def _neg_cumsum(logf_t, tb):
    b, h, t = logf_t.shape
    return pl.pallas_call(
        _cumsum_kernel,
        grid=(b, t // tb),
        in_specs=[pl.BlockSpec((None, h, tb), lambda bi, j: (bi, 0, j))],
        out_specs=pl.BlockSpec((None, h, tb), lambda bi, j: (bi, 0, j)),
        out_shape=jax.ShapeDtypeStruct((b, h, t), F32),
        scratch_shapes=[pltpu.VMEM((h, 1), F32)],
        compiler_params=_cparams(2),
        name="neg_cumsum",
    )(logf_t)


def _attn_kernel(q_ref, k_ref, v_ref, nf_ref, o_ref, m_ref, l_ref, acc_ref, *, tq, tk, past, nk):
    i = pl.program_id(2)
    j = pl.program_id(3)

    @pl.when(j == 0)
    def _():
        m_ref[...] = jnp.full_like(m_ref, NEG)
        l_ref[...] = jnp.zeros_like(l_ref)
        acc_ref[...] = jnp.zeros_like(acc_ref)

    q_lo = past + i * tq
    needed = j * tk <= q_lo + tq - 1
    unmasked = j * tk + tk - 1 <= q_lo
    lane = lax.broadcasted_iota(I32, (1, LANES), 1)

    def compute(masked):
        q = q_ref[...]
        k = k_ref[...]
        v = v_ref[...]
        for h in range(2):
            sel = (lane < HEAD_DIM) if h == 0 else (lane >= HEAD_DIM)
            qh = jnp.where(sel, q, jnp.zeros_like(q))
            s = lax.dot_general(qh, k, (((1,), (1,)), ((), ())), preferred_element_type=F32)
            s = s + nf_ref[h:h + 1, :]
            if masked:
                qpos = q_lo + lax.broadcasted_iota(I32, (tq, tk), 0)
                kpos = j * tk + lax.broadcasted_iota(I32, (tq, tk), 1)
                s = jnp.where(kpos <= qpos, s, NEG)
            m_prev = m_ref[h]
            m_new = jnp.maximum(m_prev, jnp.max(s, axis=1, keepdims=True))
            alpha = jnp.exp(m_prev - m_new)
            p = jnp.exp(s - m_new)
            l_ref[h] = alpha * l_ref[h] + jnp.sum(p, axis=1, keepdims=True)
            acc_ref[h] = alpha * acc_ref[h] + _dot(p.astype(BF16), v)
            m_ref[h] = m_new

    @pl.when(jnp.logical_and(needed, unmasked))
    def _():
        compute(False)

    @pl.when(jnp.logical_and(needed, jnp.logical_not(unmasked)))
    def _():
        compute(True)

    last_j = jnp.minimum((q_lo + tq - 1) // tk, nk - 1)

    @pl.when(j == last_j)
    def _():
        o0 = acc_ref[0] / l_ref[0]
        o1 = acc_ref[1] / l_ref[1]
        o_ref[...] = jnp.where(lane < HEAD_DIM, o0, o1).astype(o_ref.dtype)


def _attention(q, kb, vb, negf, past, tq, tk):
    b, t_q, _ = q.shape
    t_k = kb.shape[1]
    nq, nk = t_q // tq, t_k // tk
    hp = N_HEADS // 2

    def kmap(bi, p, i, j):
        return (bi, jnp.minimum(j, (past + i * tq + tq - 1) // tk), p)

    def fmap(bi, p, i, j):
        return (bi, p, 0, jnp.minimum(j, (past + i * tq + tq - 1) // tk))

    return pl.pallas_call(
        functools.partial(_attn_kernel, tq=tq, tk=tk, past=past, nk=nk),
        grid=(b, hp, nq, nk),
        in_specs=[pl.BlockSpec((None, tq, LANES), lambda bi, p, i, j: (bi, i, p)),
                  pl.BlockSpec((None, tk, LANES), kmap),
                  pl.BlockSpec((None, tk, LANES), kmap),
                  pl.BlockSpec((None, None, 2, tk), fmap)],
        out_specs=pl.BlockSpec((None, tq, LANES), lambda bi, p, i, j: (bi, i, p)),
        out_shape=jax.ShapeDtypeStruct((b, t_q, D_ATTN), BF16),
        scratch_shapes=[pltpu.VMEM((2, tq, 1), F32), pltpu.VMEM((2, tq, 1), F32), pltpu.VMEM((2, tq, LANES), F32)],
        compiler_params=_cparams(4),
        name="fox_attention",
    )(q, kb, vb, negf)


def _conv_kernel(u_ref, hist_ref, w_ref, bdw_ref, gln_ref, bln_ref, o_ref, win_ref, *, tt):
    t = pl.program_id(1)

    @pl.when(t == 0)
    def _():
        win_ref[0:HIST_PAD, :] = hist_ref[...]

    @pl.when(t > 0)
    def _():
        win_ref[0:HIST_PAD, :] = win_ref[tt:tt + HIST_PAD, :]

    win_ref[HIST_PAD:HIST_PAD + tt, :] = u_ref[...]
    off = HIST_PAD - HIST
    cols = []
    for c in range(D_CONV // LANES):
        cs = pl.ds(c * LANES, LANES)
        acc = jnp.zeros((tt, LANES), F32) + bdw_ref[:, cs]
        for tap in range(CONV_WIDTH):
            acc = acc + w_ref[tap:tap + 1, cs] * win_ref[pl.ds(off + tap, tt), cs]
        cols.append(acc)
    y = jnp.concatenate(cols, axis=1)
    mu = jnp.mean(y, axis=-1, keepdims=True)
    yc = y - mu
    var = jnp.mean(yc * yc, axis=-1, keepdims=True)
    z = yc * lax.rsqrt(var + EPS) * gln_ref[...] + bln_ref[...]
    o_ref[...] = (z * jax.nn.sigmoid(z)).astype(o_ref.dtype)


def _conv(u, hist, w_dw, b_dw, g_ln, b_ln, tt):
    b, t, c = u.shape
    return pl.pallas_call(
        functools.partial(_conv_kernel, tt=tt),
        grid=(b, t // tt),
        in_specs=[pl.BlockSpec((None, tt, c), lambda bi, ti: (bi, ti, 0)),
                  pl.BlockSpec((None, HIST_PAD, c), lambda bi, ti: (bi, 0, 0)),
                  _const_spec((HIST_PAD, c)), _const_spec((1, c)), _const_spec((1, c)), _const_spec((1, c))],
        out_specs=pl.BlockSpec((None, tt, c), lambda bi, ti: (bi, ti, 0)),
        out_shape=jax.ShapeDtypeStruct((b, t, c), BF16),
        scratch_shapes=[pltpu.VMEM((HIST_PAD + tt, c), F32)],
        compiler_params=_cparams(2),
        name="conformer_conv",
    )(u, hist, w_dw, b_dw, g_ln, b_ln)


def _out_proj_kernel(attn_ref, c_ref, x_ref, wa_ref, wc_ref, g_ref, wr_ref, br_ref, cnt_in_ref,
                     h_ref, hn_ref, topi_ref, topw_ref, rank_ref, cnt_out_ref, carry_ref):
    i = pl.program_id(0)

    @pl.when(i == 0)
    def _():
        carry_ref[...] = cnt_in_ref[...]

    h = x_ref[...] + _dot(attn_ref[...], wa_ref[...]) + _dot(c_ref[...], wc_ref[...])
    h_ref[...] = h
    ms = jnp.mean(h * h, axis=-1, keepdims=True)
    hn = h * lax.rsqrt(ms + EPS) * g_ref[...]
    hn_ref[...] = hn
    logits = _dot(hn.astype(BF16), wr_ref[...]) + br_ref[...]
    tm = logits.shape[0]
    lane = lax.broadcasted_iota(I32, (tm, LANES), 1)
    lane_f = lane.astype(F32)
    lg = jnp.where(lane < N_EXPERTS, logits, -jnp.inf)
    vals, idxs, hots = [], [], []
    for _ in range(TOP_K):
        mx = jnp.max(lg, axis=1, keepdims=True)
        idx = jnp.min(jnp.where(lg == mx, lane_f, float(LANES)), axis=1, keepdims=True)
        hot = lane_f == idx
        vals.append(mx)
        idxs.append(idx)
        hots.append(hot)
        lg = jnp.where(hot, -jnp.inf, lg)
    es = [jnp.exp(v - vals[0]) for v in vals]
    den = es[0] + es[1] + es[2] + es[3]
    picked = hots[0] | hots[1] | hots[2] | hots[3]
    sel = picked.astype(F32)
    strict = (lax.broadcasted_iota(I32, (tm, tm), 1) < lax.broadcasted_iota(I32, (tm, tm), 0)).astype(BF16)
    before = _dot(strict, sel.astype(BF16)) + carry_ref[...]
    topi = jnp.zeros((tm, LANES), I32)
    topw = jnp.zeros((tm, LANES), F32)
    rank = jnp.zeros((tm, LANES), I32)
    for k in range(TOP_K):
        rk = jnp.sum(jnp.where(hots[k], before, 0.0), axis=1, keepdims=True).astype(I32)
        topi = jnp.where(lane == k, idxs[k].astype(I32), topi)
        topw = jnp.where(lane == k, es[k] / den, topw)
        rank = jnp.where(lane == k, rk, rank)
    topi_ref[...] = topi[:, :TOP_K]
    topw_ref[...] = topw[:, :TOP_K]
    rank_ref[...] = rank[:, :TOP_K]
    carry_ref[...] = carry_ref[...] + jnp.sum(sel, axis=0, keepdims=True)
    cnt_out_ref[...] = carry_ref[...]


def _out_proj(attn, c, x, wa, wc, g_ffn, wr, br, cnt_in, tm):
    n, d = x.shape
    row = lambda w: pl.BlockSpec((tm, w), lambda i: (i, 0))
    return pl.pallas_call(
        _out_proj_kernel,
        grid=(n // tm,),
        in_specs=[row(D_ATTN), row(D_CONV), row(d), _const_spec(wa.shape), _const_spec(wc.shape),
                  _const_spec((1, d)), _const_spec(wr.shape), _const_spec((1, LANES)), _const_spec((1, LANES))],
        out_specs=[row(d), row(d), row(TOP_K), row(TOP_K), row(TOP_K), pl.BlockSpec((1, LANES), lambda i: (0, 0))],
        out_shape=[jax.ShapeDtypeStruct((n, d), F32), jax.ShapeDtypeStruct((n, d), F32),
                   jax.ShapeDtypeStruct((n, TOP_K), I32), jax.ShapeDtypeStruct((n, TOP_K), F32),
                   jax.ShapeDtypeStruct((n, TOP_K), I32), jax.ShapeDtypeStruct((1, LANES), F32)],
        scratch_shapes=[pltpu.VMEM((1, LANES), F32)],
        compiler_params=_cparams(1),
        name="out_proj_router",
    )(attn, c, x, wa, wc, g_ffn, wr, br, cnt_in)


def _row_copy(src, s, dst, d, sem):
    return pltpu.make_async_copy(src.at[pl.ds(s, 1), :], dst.at[pl.ds(d, 1), :], sem)


def _dispatch_kernel(pos_ref, hn_ref, xs_in_ref, xs_ref, sem, *, tt):
    del xs_in_ref

    def issue(r, carry):
        for k in range(TOP_K):
            _row_copy(hn_ref, r, xs_ref, pos_ref[0, r * TOP_K + k], sem).start()
        return carry

    lax.fori_loop(0, tt, issue, 0)

    def drain(r, carry):
        for k in range(TOP_K):
            _row_copy(hn_ref, 0, xs_ref, 0, sem).wait()
        return carry

    lax.fori_loop(0, tt, drain, 0)


def _dispatch(pos, hn, xs, tt):
    n, d = hn.shape
    pos3 = pos.reshape(n // tt, 1, tt * TOP_K)
    return pl.pallas_call(
        functools.partial(_dispatch_kernel, tt=tt),
        grid=(n // tt,),
        in_specs=[pl.BlockSpec((None, 1, tt * TOP_K), lambda i: (i, 0, 0), memory_space=pltpu.SMEM),
                  pl.BlockSpec((tt, d), lambda i: (i, 0)),
                  pl.BlockSpec(memory_space=pl.ANY)],
        out_specs=pl.BlockSpec(memory_space=pl.ANY),
        out_shape=jax.ShapeDtypeStruct(xs.shape, xs.dtype),
        scratch_shapes=[pltpu.SemaphoreType.DMA(())],
        input_output_aliases={2: 0},
        compiler_params=_cparams(1),
        name="moe_dispatch",
    )(pos3, hn, xs)


def _moe_a_kernel(te_ref, nt_ref, x_ref, wg_ref, wu_ref, bg_ref, bu_ref, o_ref, *, cw):
    i = pl.program_id(0)

    @pl.when(i < nt_ref[0])
    def _():
        x = x_ref[...].astype(BF16)
        for c in range(x_ref.shape[1] // cw):
            cs = pl.ds(c * cw, cw)
            gt = jnp.minimum(_dot(x, wg_ref[:, cs]) + bg_ref[:, cs], SWIGLU_LIMIT)
            up = jnp.clip(_dot(x, wu_ref[:, cs]) + bu_ref[:, cs], -SWIGLU_LIMIT, SWIGLU_LIMIT)
            hid = (up + 1.0) * (gt * jax.nn.sigmoid(SWIGLU_ALPHA * gt))
            o_ref[:, cs] = hid.astype(o_ref.dtype)

    @pl.when(i >= nt_ref[0])
    def _():
        o_ref[...] = jnp.zeros_like(o_ref)


def _moe_a(te, nt, xs, wg, wu, bg, bu, tm):
    p, d = xs.shape
    dff = wg.shape[2]
    wspec = pl.BlockSpec((None, d, dff), lambda i, te, nt: (te[i], 0, 0))
    bspec = pl.BlockSpec((None, 1, dff), lambda i, te, nt: (te[i], 0, 0))
    return pl.pallas_call(
        functools.partial(_moe_a_kernel, cw=512),
        grid_spec=pltpu.PrefetchScalarGridSpec(
            num_scalar_prefetch=2, grid=(p // tm,),
            in_specs=[pl.BlockSpec((tm, d), lambda i, te, nt: (i, 0)), wspec, wspec, bspec, bspec],
            out_specs=pl.BlockSpec((tm, dff), lambda i, te, nt: (i, 0))),
        out_shape=jax.ShapeDtypeStruct((p, dff), BF16),
        compiler_params=_cparams(1),
        name="moe_gate_up",
    )(te, nt, xs, wg, wu, bg, bu)


def _moe_b_kernel(te_ref, nt_ref, h_ref, wd_ref, bd_ref, o_ref):
    i = pl.program_id(0)

    @pl.when(i < nt_ref[0])
    def _():
        o_ref[...] = _dot(h_ref[...], wd_ref[...]) + bd_ref[...]

    @pl.when(i >= nt_ref[0])
    def _():
        o_ref[...] = jnp.zeros_like(o_ref)


def _moe_b(te, nt, hid, wd, bd, tm):
    p, dff = hid.shape
    d = wd.shape[2]
    return pl.pallas_call(
        _moe_b_kernel,
        grid_spec=pltpu.PrefetchScalarGridSpec(
            num_scalar_prefetch=2, grid=(p // tm,),
            in_specs=[pl.BlockSpec((tm, dff), lambda i, te, nt: (i, 0)),
                      pl.BlockSpec((None, dff, d), lambda i, te, nt: (te[i], 0, 0)),
                      pl.BlockSpec((None, 1, d), lambda i, te, nt: (te[i], 0, 0))],
            out_specs=pl.BlockSpec((tm, d), lambda i, te, nt: (i, 0))),
        out_shape=jax.ShapeDtypeStruct((p, d), F32),
        compiler_params=_cparams(1),
        name="moe_down",
    )(te, nt, hid, wd, bd)


def _combine_kernel(pos_ref, h_ref, w_ref, g_ref, ys_ref, y_ref, gath_ref, sem, *, tt):
    def issue(r, carry):
        for k in range(TOP_K):
            _row_copy(ys_ref, pos_ref[0, r * TOP_K + k], gath_ref.at[k], r, sem).start()
        return carry

    lax.fori_loop(0, tt, issue, 0)

    def drain(r, carry):
        for k in range(TOP_K):
            _row_copy(ys_ref, 0, gath_ref.at[k], 0, sem).wait()
        return carry

    lax.fori_loop(0, tt, drain, 0)
    w = w_ref[...]
    moe = w[:, 0:1] * gath_ref[0]
    for k in range(1, TOP_K):
        moe = moe + w[:, k:k + 1] * gath_ref[k]
    h2 = h_ref[...] + moe
    ms = jnp.mean(h2 * h2, axis=-1, keepdims=True)
    y_ref[...] = h2 * lax.rsqrt(ms + EPS) * g_ref[...]


def _combine(pos, h, topw, g_final, ys, tt):
    n, d = h.shape
    pos3 = pos.reshape(n // tt, 1, tt * TOP_K)
    return pl.pallas_call(
        functools.partial(_combine_kernel, tt=tt),
        grid=(n // tt,),
        in_specs=[pl.BlockSpec((None, 1, tt * TOP_K), lambda i: (i, 0, 0), memory_space=pltpu.SMEM),
                  pl.BlockSpec((tt, d), lambda i: (i, 0)),
                  pl.BlockSpec((tt, TOP_K), lambda i: (i, 0)),
                  _const_spec((1, d)),
                  pl.BlockSpec(memory_space=pl.ANY)],
        out_specs=pl.BlockSpec((tt, d), lambda i: (i, 0)),
        out_shape=jax.ShapeDtypeStruct((n, d), F32),
        scratch_shapes=[pltpu.VMEM((TOP_K, tt, d), F32), pltpu.SemaphoreType.DMA(())],
        compiler_params=_cparams(1),
        name="moe_combine",
    )(pos3, h, topw, g_final, ys)


def _row_tile(n, want):
    t = min(n, want)
    assert n % t == 0, (n, t)
    return t


MOE_TILE = 256


def _mixer(x3, k_past, v_past, logf_past, conv_past, W, cnt_in):
    b, t, d = x3.shape
    n = b * t
    past = 0 if k_past is None else k_past.shape[1]
    x = x3.reshape(n, d)
    q, k, v, kb, vb, logf, u = _in_proj(x, W["g_mix"], W["wq"], W["wk"], W["wv"], W["wf"], W["wa"], W["wg"],
                                        W["bf"], _row_tile(n, 256))
    logf3 = logf.reshape(b, t, N_HEADS)
    kb3, vb3 = kb.reshape(b, t, D_ATTN), vb.reshape(b, t, D_ATTN)
    if past:
        logf_all = jnp.concatenate([logf_past.astype(F32), logf3], axis=1)
        kb3 = jnp.concatenate([k_past.reshape(b, past, D_ATTN).astype(BF16), kb3], axis=1)
        vb3 = jnp.concatenate([v_past.reshape(b, past, D_ATTN).astype(BF16), vb3], axis=1)
        hist = jnp.pad(conv_past, ((0, 0), (HIST_PAD - HIST, 0), (0, 0)))
    else:
        logf_all = logf3
        hist = jnp.zeros((b, HIST_PAD, D_CONV), F32)
    t_k = past + t
    tb = 512 if t_k % 512 == 0 else t_k
    negf = _neg_cumsum(logf_all.transpose(0, 2, 1), tb).reshape(b, N_HEADS // 2, 2, t_k)
    if t_k % 1024 == 0 and t % 1024 == 0:
        tq = tk = 1024
    elif t_k % 128 == 0 and t % 128 == 0 and past == 0:
        tq = tk = 128
    else:
        tq, tk = t, t_k
    attn = _attention(q.reshape(b, t, D_ATTN), kb3, vb3, negf, past, tq, tk)
    u3 = u.reshape(b, t, D_CONV)
    c = _conv(u3, hist, W["w_dw"], W["b_dw"], W["g_ln"], W["b_ln"], _row_tile(t, 256))
    h, hn, topi, topw, rank, cnt = _out_proj(attn.reshape(n, D_ATTN), c.reshape(n, D_CONV), x, W["wo_a"], W["wo_c"],
                                             W["g_ffn"], W["wr"], W["br"], cnt_in, _row_tile(n, 256))
    new_conv = u3[:, t - HIST:] if t >= HIST else jnp.concatenate([hist[:, HIST_PAD - HIST:], u3], axis=1)[:, t:]
    outs = (k.reshape(b, t, N_HEADS, HEAD_DIM), v.reshape(b, t, N_HEADS, HEAD_DIM), logf3, new_conv)
    return (h, hn, topi, topw, rank, cnt), outs


def kernel(x_prompt, x_sample, cache_k, cache_v, cache_logf, cache_conv, g_norm_mix, w_in, b_f, w_dw, b_dw, g_ln,
           b_ln, w_out, g_norm_ffn, w_router, b_router, w_gate, b_gate, w_up, b_up, w_down, b_down, g_norm_final):
    d = x_prompt.shape[-1]
    row = lambda a: a.reshape(1, -1).astype(F32)
    wi = w_in.astype(BF16)
    s0, s1, s2, s3, s4 = D_ATTN, 2 * D_ATTN, 3 * D_ATTN, 3 * D_ATTN + N_HEADS, 3 * D_ATTN + N_HEADS + D_CONV
    wo = w_out.astype(BF16)
    W = {
        "g_mix": row(g_norm_mix),
        "wq": wi[:, :s0], "wk": wi[:, s0:s1], "wv": wi[:, s1:s2],
        "wf": jnp.pad(wi[:, s2:s3], ((0, 0), (0, LANES - N_HEADS))),
        "wa": wi[:, s3:s4], "wg": wi[:, s4:],
        "bf": jnp.pad(row(b_f), ((0, 0), (0, LANES - N_HEADS))),
        "w_dw": jnp.pad(w_dw.astype(F32), ((0, HIST_PAD - CONV_WIDTH), (0, 0))),
        "b_dw": row(b_dw), "g_ln": row(g_ln), "b_ln": row(b_ln),
        "wo_a": wo[:D_ATTN], "wo_c": wo[D_ATTN:],
        "g_ffn": row(g_norm_ffn),
        "wr": jnp.pad(w_router.astype(BF16), ((0, 0), (0, LANES - N_EXPERTS))),
        "br": jnp.pad(row(b_router), ((0, 0), (0, LANES - N_EXPERTS))),
    }
    zero_cnt = jnp.zeros((1, LANES), F32)
    (h_p, hn_p, ti_p, tw_p, rk_p, cnt_p), outs_p = _mixer(x_prompt, None, None, None, None, W, zero_cnt)
    (h_s, hn_s, ti_s, tw_s, rk_s, cnt), outs_s = _mixer(x_sample, cache_k, cache_v, cache_logf, cache_conv, W, cnt_p)

    n_p, n_s = h_p.shape[0], h_s.shape[0]
    tmm = MOE_TILE
    n_tiles = (n_p + n_s) * TOP_K // tmm + N_EXPERTS
    counts = cnt[0, :N_EXPERTS].astype(I32)
    padded = (counts + tmm - 1) // tmm * tmm
    ends = jnp.cumsum(padded)
    offsets = ends - padded
    pos_p = offsets[ti_p] + rk_p
    pos_s = offsets[ti_s] + rk_s
    nt = (ends[-1] // tmm).reshape(1).astype(I32)
    te = jnp.minimum(jnp.searchsorted(ends, jnp.arange(n_tiles, dtype=I32) * tmm, side="right"),
                     N_EXPERTS - 1).astype(I32)

    xs = jnp.zeros((n_tiles * tmm, d), F32)
    xs = _dispatch(pos_p, hn_p, xs, _row_tile(n_p, 256))
    xs = _dispatch(pos_s, hn_s, xs, _row_tile(n_s, 256))
    hid = _moe_a(te, nt, xs, w_gate.astype(BF16), w_up.astype(BF16), b_gate[:, None, :], b_up[:, None, :], tmm)
    ys = _moe_b(te, nt, hid, w_down.astype(BF16), b_down[:, None, :], tmm)
    g_fin = row(g_norm_final)
    y_p = _combine(pos_p, h_p, tw_p, g_fin, ys, _row_tile(n_p, 256))
    y_s = _combine(pos_s, h_s, tw_s, g_fin, ys, _row_tile(n_s, 256))
    return (y_p.reshape(x_prompt.shape), y_s.reshape(x_sample.shape)) + outs_p + outs_s
```

```python
import functools

import numpy as np
import jax
import jax.numpy as jnp
from jax import lax
from jax.experimental import pallas as pl
from jax.experimental.pallas import tpu as pltpu

F32 = jnp.float32
BF16 = jnp.bfloat16
I32 = jnp.int32

D_ATTN = 1024
N_HEADS = 16
HEAD_DIM = 64
D_CONV = 1024
CONV_WIDTH = 31
HIST = CONV_WIDTH - 1
N_EXPERTS = 32
TOP_K = 4
SWIGLU_LIMIT = 7.0
SWIGLU_ALPHA = 1.702
EPS = 1e-5
NEG = -1e30

LANES = 128
MXU_DIM = 256
HIST_PAD = 32
N_SPLIT = 3
VMEM_LIMIT = 56 * 1024 * 1024


def _cparams(n_axes, vmem=VMEM_LIMIT):
    return pltpu.CompilerParams(dimension_semantics=("arbitrary",) * n_axes, vmem_limit_bytes=vmem)


def _dot(a, b):
    return jnp.dot(a, b, preferred_element_type=F32)


def _const_spec(shape):
    nd = len(shape)
    return pl.BlockSpec(shape, lambda *_: (0,) * nd, pipeline_mode=pl.Buffered(1))


def _split3(x):
    x1 = x.astype(BF16)
    r1 = x - x1.astype(F32)
    x2 = r1.astype(BF16)
    x3 = (r1 - x2.astype(F32)).astype(BF16)
    return x1, x2, x3


def _in_proj_kernel(x_ref, g_ref, wq_ref, wk_ref, wv_ref, wf_ref, wa_ref, wg_ref, bf_ref,
                    q_ref, k_ref, v_ref, kb_ref, vb_ref, logf_ref, u_ref, *, transposed):
    x = x_ref[...]
    ms = jnp.mean(x * x, axis=-1, keepdims=True)
    xn = (x * lax.rsqrt(ms + EPS) * g_ref[...]).astype(BF16)
    q = _dot(xn, wq_ref[...]) * (HEAD_DIM ** -0.5)
    q_ref[...] = (q.T if transposed else q).astype(BF16)
    k = _dot(xn, wk_ref[...])
    k_ref[...] = k
    kb_ref[...] = k.astype(BF16)
    v = _dot(xn, wv_ref[...])
    v_ref[...] = v
    vb_ref[...] = (v.T if transposed else v).astype(BF16)
    fl = _dot(xn, wf_ref[...]) + bf_ref[...]
    logf = jnp.minimum(fl, 0.0) - jnp.log1p(jnp.exp(-jnp.abs(fl)))
    logf_ref[...] = logf[:, :N_HEADS]
    a = _dot(xn, wa_ref[...])
    g = _dot(xn, wg_ref[...])
    u_ref[...] = a * jax.nn.sigmoid(g)


def _in_proj(x, g_mix, wq, wk, wv, wf, wa, wg, bf_pad, tm, seq_len, transposed):
    n, d = x.shape
    row = lambda w: pl.BlockSpec((tm, w), lambda i: (i, 0))
    if transposed:
        tpb = seq_len // tm
        tspec = pl.BlockSpec((None, D_ATTN, tm), lambda i: (i // tpb, 0, i % tpb))
        tshape = jax.ShapeDtypeStruct((n // seq_len, D_ATTN, seq_len), BF16)
    else:
        tspec, tshape = row(D_ATTN), jax.ShapeDtypeStruct((n, D_ATTN), BF16)
    return pl.pallas_call(
        functools.partial(_in_proj_kernel, transposed=transposed),
        grid=(n // tm,),
        in_specs=[row(d), _const_spec((1, d)), _const_spec(wq.shape), _const_spec(wk.shape), _const_spec(wv.shape),
                  _const_spec(wf.shape), _const_spec(wa.shape), _const_spec(wg.shape), _const_spec((1, LANES))],
        out_specs=[tspec, row(D_ATTN), row(D_ATTN), row(D_ATTN), tspec, row(N_HEADS), row(D_CONV)],
        out_shape=[tshape, jax.ShapeDtypeStruct((n, D_ATTN), F32),
                   jax.ShapeDtypeStruct((n, D_ATTN), F32), jax.ShapeDtypeStruct((n, D_ATTN), BF16),
                   tshape, jax.ShapeDtypeStruct((n, N_HEADS), F32),
                   jax.ShapeDtypeStruct((n, D_CONV), F32)],
        compiler_params=_cparams(1),
        name="in_proj",
    )(x, g_mix, wq, wk, wv, wf, wa, wg, bf_pad)


def _aug_selectors():
    sel = np.zeros((N_SPLIT, N_HEADS, D_ATTN), np.float32)
    for c in range(N_SPLIT):
        for h in range(N_HEADS):
            sel[c, h, (h // 2) * LANES + N_SPLIT * (h % 2) + c] = 1.0
    return jnp.asarray(sel, BF16)


def _forget_aug_kernel(x_ref, sel_ref, o_ref, carry_ref):
    j = pl.program_id(1)

    @pl.when(j == 0)
    def _():
        carry_ref[...] = jnp.zeros_like(carry_ref)

    x = x_ref[...]
    tb = x.shape[0]
    tri = (lax.broadcasted_iota(I32, (tb, tb), 1) <= lax.broadcasted_iota(I32, (tb, tb), 0)).astype(BF16)
    x1, x2, x3 = _split3(x)
    f = _dot(tri, x1) + _dot(tri, x2) + _dot(tri, x3) + carry_ref[...]
    carry_ref[...] = f[tb - 1:tb, :]
    n1, n2, n3 = _split3(-f)
    o_ref[...] = (_dot(n1, sel_ref[0]) + _dot(n2, sel_ref[1]) + _dot(n3, sel_ref[2])).astype(o_ref.dtype)


def _forget_aug(logf, tb):
    b, t, h = logf.shape
    return pl.pallas_call(
        _forget_aug_kernel,
        grid=(b, t // tb),
        in_specs=[pl.BlockSpec((None, tb, h), lambda bi, j: (bi, j, 0)), _const_spec((N_SPLIT, h, D_ATTN))],
        out_specs=pl.BlockSpec((None, tb, D_ATTN), lambda bi, j: (bi, j, 0)),
        out_shape=jax.ShapeDtypeStruct((b, t, D_ATTN), BF16),
        scratch_shapes=[pltpu.VMEM((1, h), F32)],
        compiler_params=_cparams(2),
        name="forget_aug",
    )(logf, _aug_selectors())


def _attn_kernel(si_ref, sj_ref, qt_ref, k_ref, a_ref, vt_ref, o_ref, m_ref, mn_ref, acc_ref, s_ref, p_ref,
                 *, tq, tk, past, nk, cw, aligned):
    step = pl.program_id(2)
    i = si_ref[step]
    j = sj_ref[step]
    q_lo = past + i * tq
    jmax = jnp.minimum((q_lo + tq - 1) // tk, nk - 1)

    @pl.when(j == 0)
    def _():
        m_ref[...] = jnp.full_like(m_ref, NEG)
        acc_ref[...] = jnp.zeros_like(acc_ref)

    unmasked = j * tk + tk - 1 <= q_lo
    row = lax.broadcasted_iota(I32, (LANES, 1), 0)

    def ind(cond):
        return jnp.where(cond, 1.0, 0.0).astype(BF16)

    rb = 64

    def compute(masked):
        kaug = jnp.concatenate([k_ref[...], a_ref[...]], axis=1)
        vt = vt_ref[...]
        row_k = lax.broadcasted_iota(I32, (LANES, tk), 0)
        row_q = lax.broadcasted_iota(I32, (LANES, cw), 0)
        chunks = [(h, c, min(tk, (c + 1) * cw) if (masked and aligned) else tk)
                  for h in range(2) for c in range(tq // cw)]
        q_keep = [ind((row_q >= HEAD_DIM * h) & (row_q < HEAD_DIM * (h + 1))) for h in range(2)]
        sel_rows = [ind((row_q >= N_SPLIT * h) & (row_q < N_SPLIT * (h + 1))) for h in range(2)]
        vth = [vt * ind((row_k >= HEAD_DIM * h) & (row_k < HEAD_DIM * (h + 1))) + ind(row_k == HEAD_DIM * (1 - h))
               for h in range(2)]
        for h, c, rows in chunks:
            cs = pl.ds(c * cw, cw)
            qa = jnp.concatenate([qt_ref[:, cs] * q_keep[h], sel_rows[h]], axis=0)
            s = _dot(kaug[:rows], qa)
            if masked:
                kpos = j * tk + lax.broadcasted_iota(I32, (rows, cw), 0)
                qpos = q_lo + c * cw + lax.broadcasted_iota(I32, (rows, cw), 1)
                s = jnp.where(kpos <= qpos, s, NEG)
            s_ref[h, 0:rows, cs] = s
            mn_ref[h, :, cs] = jnp.maximum(m_ref[h, :, cs], jnp.max(s, axis=0, keepdims=True))
        for h, c, rows in chunks:
            cs = pl.ds(c * cw, cw)
            p_ref[h, 0:rows, cs] = jnp.exp(s_ref[h, 0:rows, cs] - mn_ref[h, :, cs]).astype(BF16)
        for h, c, rows in chunks:
            cs = pl.ds(c * cw, cw)
            alpha = jnp.exp(m_ref[h, :, cs] - mn_ref[h, :, cs])
            acc_ref[h, :, cs] = alpha * acc_ref[h, :, cs] + _dot(vth[h][:, :rows], p_ref[h, 0:rows, cs])
            m_ref[h, :, cs] = mn_ref[h, :, cs]

    @pl.when(unmasked)
    def _():
        compute(False)

    @pl.when(jnp.logical_not(unmasked))
    def _():
        compute(True)

    @pl.when(j == jmax)
    def _():
        a0 = acc_ref[0]
        a1 = acc_ref[1]
        ot = jnp.where(row < HEAD_DIM, a0 / a0[HEAD_DIM:HEAD_DIM + 1, :], a1 / a1[0:1, :])
        o_ref[...] = ot.T.astype(o_ref.dtype)


def _attention(qt, kb, aug, vt, past, tq, tk):
    b, _, t_q = qt.shape
    t_k = kb.shape[1]
    nq, nk = t_q // tq, t_k // tk
    hp = N_HEADS // 2
    steps = [(i, j) for i in range(nq) for j in range(min((past + i * tq + tq - 1) // tk, nk - 1) + 1)]
    si = jnp.asarray(np.array([s[0] for s in steps], np.int32))
    sj = jnp.asarray(np.array([s[1] for s in steps], np.int32))
    cw = min(tq, MXU_DIM)
    return pl.pallas_call(
        functools.partial(_attn_kernel, tq=tq, tk=tk, past=past, nk=nk, cw=cw, aligned=(past == 0 and tq == tk)),
        grid_spec=pltpu.PrefetchScalarGridSpec(
            num_scalar_prefetch=2, grid=(b, hp, len(steps)),
            in_specs=[pl.BlockSpec((None, LANES, tq), lambda bi, p, s, si, sj: (bi, p, si[s])),
                      pl.BlockSpec((None, tk, LANES), lambda bi, p, s, si, sj: (bi, sj[s], p)),
                      pl.BlockSpec((None, tk, LANES), lambda bi, p, s, si, sj: (bi, sj[s], p)),
                      pl.BlockSpec((None, LANES, tk), lambda bi, p, s, si, sj: (bi, p, sj[s]))],
            out_specs=pl.BlockSpec((None, tq, LANES), lambda bi, p, s, si, sj: (bi, si[s], p)),
            scratch_shapes=[pltpu.VMEM((2, 1, tq), F32), pltpu.VMEM((2, 1, tq), F32), pltpu.VMEM((2, LANES, tq), F32),
                            pltpu.VMEM((2, tk, tq), F32), pltpu.VMEM((2, tk, tq), BF16)]),
        out_shape=jax.ShapeDtypeStruct((b, t_q, D_ATTN), BF16),
        compiler_params=_cparams(3),
        name="fox_attention",
    )(si, sj, qt, kb, aug, vt)


def _conv_kernel(u_ref, hist_ref, w_ref, bdw_ref, gln_ref, bln_ref, o_ref, win_ref, *, tt):
    t = pl.program_id(1)

    @pl.when(t == 0)
    def _():
        win_ref[0:HIST_PAD, :] = hist_ref[...]

    @pl.when(t > 0)
    def _():
        win_ref[0:HIST_PAD, :] = win_ref[tt:tt + HIST_PAD, :]

    win_ref[HIST_PAD:HIST_PAD + tt, :] = u_ref[...]
    off = HIST_PAD - HIST
    cols = []
    for c in range(D_CONV // LANES):
        cs = pl.ds(c * LANES, LANES)
        acc = jnp.zeros((tt, LANES), F32) + bdw_ref[:, cs]
        for tap in range(CONV_WIDTH):
            acc = acc + w_ref[tap:tap + 1, cs] * win_ref[pl.ds(off + tap, tt), cs]
        cols.append(acc)
    y = jnp.concatenate(cols, axis=1)
    mu = jnp.mean(y, axis=-1, keepdims=True)
    yc = y - mu
    var = jnp.mean(yc * yc, axis=-1, keepdims=True)
    z = yc * lax.rsqrt(var + EPS) * gln_ref[...] + bln_ref[...]
    o_ref[...] = (z * jax.nn.sigmoid(z)).astype(o_ref.dtype)


def _conv(u, hist, w_dw, b_dw, g_ln, b_ln, tt):
    b, t, c = u.shape
    return pl.pallas_call(
        functools.partial(_conv_kernel, tt=tt),
        grid=(b, t // tt),
        in_specs=[pl.BlockSpec((None, tt, c), lambda bi, ti: (bi, ti, 0)),
                  pl.BlockSpec((None, HIST_PAD, c), lambda bi, ti: (bi, 0, 0)),
                  _const_spec((HIST_PAD, c)), _const_spec((1, c)), _const_spec((1, c)), _const_spec((1, c))],
        out_specs=pl.BlockSpec((None, tt, c), lambda bi, ti: (bi, ti, 0)),
        out_shape=jax.ShapeDtypeStruct((b, t, c), BF16),
        scratch_shapes=[pltpu.VMEM((HIST_PAD + tt, c), F32)],
        compiler_params=_cparams(2),
        name="conformer_conv",
    )(u, hist, w_dw, b_dw, g_ln, b_ln)


def _out_proj_kernel(attn_ref, c_ref, x_ref, wa_ref, wc_ref, g_ref, wr_ref, br_ref, cnt_in_ref,
                     h_ref, hn_ref, topi_ref, topw_ref, rank_ref, cnt_out_ref, carry_ref):
    i = pl.program_id(0)

    @pl.when(i == 0)
    def _():
        carry_ref[...] = cnt_in_ref[...]

    h = x_ref[...] + _dot(attn_ref[...], wa_ref[...]) + _dot(c_ref[...], wc_ref[...])
    h_ref[...] = h
    ms = jnp.mean(h * h, axis=-1, keepdims=True)
    hn = h * lax.rsqrt(ms + EPS) * g_ref[...]
    hn_ref[...] = hn
    logits = _dot(hn.astype(BF16), wr_ref[...]) + br_ref[...]
    tm = logits.shape[0]
    lane = lax.broadcasted_iota(I32, (tm, LANES), 1)
    lane_f = lane.astype(F32)
    lg = jnp.where(lane < N_EXPERTS, logits, -jnp.inf)
    vals, idxs, hots = [], [], []
    for _ in range(TOP_K):
        mx = jnp.max(lg, axis=1, keepdims=True)
        idx = jnp.min(jnp.where(lg == mx, lane_f, float(LANES)), axis=1, keepdims=True)
        hot = lane_f == idx
        vals.append(mx)
        idxs.append(idx)
        hots.append(hot)
        lg = jnp.where(hot, -jnp.inf, lg)
    es = [jnp.exp(v - vals[0]) for v in vals]
    den = es[0] + es[1] + es[2] + es[3]
    picked = hots[0] | hots[1] | hots[2] | hots[3]
    sel = picked.astype(F32)
    strict = (lax.broadcasted_iota(I32, (tm, tm), 1) < lax.broadcasted_iota(I32, (tm, tm), 0)).astype(BF16)
    before = _dot(strict, sel.astype(BF16)) + carry_ref[...]
    topi = jnp.zeros((tm, LANES), I32)
    topw = jnp.zeros((tm, LANES), F32)
    rank = jnp.zeros((tm, LANES), I32)
    for k in range(TOP_K):
        rk = jnp.sum(jnp.where(hots[k], before, 0.0), axis=1, keepdims=True).astype(I32)
        topi = jnp.where(lane == k, idxs[k].astype(I32), topi)
        topw = jnp.where(lane == k, es[k] / den, topw)
        rank = jnp.where(lane == k, rk, rank)
    topi_ref[...] = topi[:, :TOP_K]
    topw_ref[...] = topw[:, :TOP_K]
    rank_ref[...] = rank[:, :TOP_K]
    carry_ref[...] = carry_ref[...] + jnp.sum(sel, axis=0, keepdims=True)
    cnt_out_ref[...] = carry_ref[...]


def _out_proj(attn, c, x, wa, wc, g_ffn, wr, br, cnt_in, tm):
    n, d = x.shape
    row = lambda w: pl.BlockSpec((tm, w), lambda i: (i, 0))
    return pl.pallas_call(
        _out_proj_kernel,
        grid=(n // tm,),
        in_specs=[row(D_ATTN), row(D_CONV), row(d), _const_spec(wa.shape), _const_spec(wc.shape),
                  _const_spec((1, d)), _const_spec(wr.shape), _const_spec((1, LANES)), _const_spec((1, LANES))],
        out_specs=[row(d), row(d), row(TOP_K), row(TOP_K), row(TOP_K), pl.BlockSpec((1, LANES), lambda i: (0, 0))],
        out_shape=[jax.ShapeDtypeStruct((n, d), F32), jax.ShapeDtypeStruct((n, d), F32),
                   jax.ShapeDtypeStruct((n, TOP_K), I32), jax.ShapeDtypeStruct((n, TOP_K), F32),
                   jax.ShapeDtypeStruct((n, TOP_K), I32), jax.ShapeDtypeStruct((1, LANES), F32)],
        scratch_shapes=[pltpu.VMEM((1, LANES), F32)],
        compiler_params=_cparams(1),
        name="out_proj_router",
    )(attn, c, x, wa, wc, g_ffn, wr, br, cnt_in)


def _row_copy(src, s, dst, d, sem):
    return pltpu.make_async_copy(src.at[pl.ds(s, 1), :], dst.at[pl.ds(d, 1), :], sem)


def _zero_pad_rows(start_ref, count_ref, zero_ref, xs_ref, sem, wait):
    def per_expert(e, carry):
        start = start_ref[e]

        def per_row(r, c2):
            cp = _row_copy(zero_ref, 0, xs_ref, start + r, sem)
            if wait:
                cp.wait()
            else:
                cp.start()
            return c2

        lax.fori_loop(0, count_ref[e], per_row, 0)
        return carry

    lax.fori_loop(0, N_EXPERTS, per_expert, 0)


def _zero_tail_tiles(nt_ref, zero_ref, xs_ref, sem, n_tiles, tile_rows, wait):
    def per_tile(t, carry):
        dst = xs_ref.at[pl.ds(pl.multiple_of(t * tile_rows, tile_rows), tile_rows), :]
        cp = pltpu.make_async_copy(zero_ref, dst, sem)
        if wait:
            cp.wait()
        else:
            cp.start()
        return carry

    lax.fori_loop(nt_ref[0], n_tiles, per_tile, 0)


def _dispatch_kernel(pos_ref, zs_ref, zn_ref, nt_ref, hp_ref, hs_ref, xs_ref, sem, zero_ref,
                     *, tt, np_tiles, n_tiles, tile_rows):
    i = pl.program_id(0)

    @pl.when(i == 0)
    def _():
        zero_ref[...] = jnp.zeros_like(zero_ref)
        _zero_pad_rows(zs_ref, zn_ref, zero_ref, xs_ref, sem, wait=False)
        _zero_tail_tiles(nt_ref, zero_ref, xs_ref, sem, n_tiles, tile_rows, wait=False)
        _zero_pad_rows(zs_ref, zn_ref, zero_ref, xs_ref, sem, wait=True)
        _zero_tail_tiles(nt_ref, zero_ref, xs_ref, sem, n_tiles, tile_rows, wait=True)

    def scatter(hn_ref):
        def issue(r, carry):
            for k in range(TOP_K):
                _row_copy(hn_ref, r, xs_ref, pos_ref[0, r * TOP_K + k], sem).start()
            return carry

        lax.fori_loop(0, tt, issue, 0)

        def drain(r, carry):
            for k in range(TOP_K):
                _row_copy(hn_ref, 0, xs_ref, 0, sem).wait()
            return carry

        lax.fori_loop(0, tt, drain, 0)

    @pl.when(i < np_tiles)
    def _():
        scatter(hp_ref)

    @pl.when(i >= np_tiles)
    def _():
        scatter(hs_ref)


def _dispatch(pos, hn_p, hn_s, pad_start, pad_count, nt, n_tiles, tile_rows, tt):
    n_p, d = hn_p.shape
    n_s = hn_s.shape[0]
    np_tiles, ns_tiles = n_p // tt, n_s // tt
    pos3 = pos.reshape(np_tiles + ns_tiles, 1, tt * TOP_K)
    smem = pl.BlockSpec(memory_space=pltpu.SMEM)
    return pl.pallas_call(
        functools.partial(_dispatch_kernel, tt=tt, np_tiles=np_tiles, n_tiles=n_tiles, tile_rows=tile_rows),
        grid=(np_tiles + ns_tiles,),
        in_specs=[pl.BlockSpec((None, 1, tt * TOP_K), lambda i: (i, 0, 0), memory_space=pltpu.SMEM), smem, smem, smem,
                  pl.BlockSpec((tt, d), lambda i: (jnp.minimum(i, np_tiles - 1), 0)),
                  pl.BlockSpec((tt, d), lambda i: (jnp.maximum(i - np_tiles, 0), 0))],
        out_specs=pl.BlockSpec(memory_space=pl.ANY),
        out_shape=jax.ShapeDtypeStruct((n_tiles * tile_rows, d), F32),
        scratch_shapes=[pltpu.SemaphoreType.DMA(()), pltpu.VMEM((tile_rows, d), F32)],
        compiler_params=_cparams(1),
        name="moe_dispatch",
    )(pos3, pad_start, pad_count, nt, hn_p, hn_s)


def _moe_a_kernel(te_ref, nt_ref, x_ref, wg_ref, wu_ref, bg_ref, bu_ref, o_ref, *, cw):
    i = pl.program_id(0)

    @pl.when(i < nt_ref[0])
    def _():
        x = x_ref[...].astype(BF16)
        for c in range(x_ref.shape[1] // cw):
            cs = pl.ds(c * cw, cw)
            gt = jnp.minimum(_dot(x, wg_ref[:, cs]) + bg_ref[:, cs], SWIGLU_LIMIT)
            up = jnp.clip(_dot(x, wu_ref[:, cs]) + bu_ref[:, cs], -SWIGLU_LIMIT, SWIGLU_LIMIT)
            hid = (up + 1.0) * (gt * jax.nn.sigmoid(SWIGLU_ALPHA * gt))
            o_ref[:, cs] = hid.astype(o_ref.dtype)

    @pl.when(i >= nt_ref[0])
    def _():
        o_ref[...] = jnp.zeros_like(o_ref)


def _moe_a(te, nt, xs, wg, wu, bg, bu, tm):
    p, d = xs.shape
    dff = wg.shape[2]
    wspec = pl.BlockSpec((None, d, dff), lambda i, te, nt: (te[i], 0, 0))
    bspec = pl.BlockSpec((None, 1, dff), lambda i, te, nt: (te[i], 0, 0))
    return pl.pallas_call(
        functools.partial(_moe_a_kernel, cw=512),
        grid_spec=pltpu.PrefetchScalarGridSpec(
            num_scalar_prefetch=2, grid=(p // tm,),
            in_specs=[pl.BlockSpec((tm, d), lambda i, te, nt: (i, 0)), wspec, wspec, bspec, bspec],
            out_specs=pl.BlockSpec((tm, dff), lambda i, te, nt: (i, 0))),
        out_shape=jax.ShapeDtypeStruct((p, dff), BF16),
        compiler_params=_cparams(1),
        name="moe_gate_up",
    )(te, nt, xs, wg, wu, bg, bu)


def _moe_b_kernel(te_ref, nt_ref, h_ref, wd_ref, bd_ref, o_ref):
    i = pl.program_id(0)

    @pl.when(i < nt_ref[0])
    def _():
        o_ref[...] = _dot(h_ref[...], wd_ref[...]) + bd_ref[...]

    @pl.when(i >= nt_ref[0])
    def _():
        o_ref[...] = jnp.zeros_like(o_ref)


def _moe_b(te, nt, hid, wd, bd, tm):
    p, dff = hid.shape
    d = wd.shape[2]
    return pl.pallas_call(
        _moe_b_kernel,
        grid_spec=pltpu.PrefetchScalarGridSpec(
            num_scalar_prefetch=2, grid=(p // tm,),
            in_specs=[pl.BlockSpec((tm, dff), lambda i, te, nt: (i, 0)),
                      pl.BlockSpec((None, dff, d), lambda i, te, nt: (te[i], 0, 0)),
                      pl.BlockSpec((None, 1, d), lambda i, te, nt: (te[i], 0, 0))],
            out_specs=pl.BlockSpec((tm, d), lambda i, te, nt: (i, 0))),
        out_shape=jax.ShapeDtypeStruct((p, d), F32),
        compiler_params=_cparams(1),
        name="moe_down",
    )(te, nt, hid, wd, bd)


def _combine_kernel(pos_ref, h_ref, w_ref, g_ref, ys_ref, y_ref, gath_ref, sem, *, tt):
    def issue(r, carry):
        for k in range(TOP_K):
            _row_copy(ys_ref, pos_ref[0, r * TOP_K + k], gath_ref.at[k], r, sem).start()
        return carry

    lax.fori_loop(0, tt, issue, 0)

    def drain(r, carry):
        for k in range(TOP_K):
            _row_copy(ys_ref, 0, gath_ref.at[k], 0, sem).wait()
        return carry

    lax.fori_loop(0, tt, drain, 0)
    w = w_ref[...]
    moe = w[:, 0:1] * gath_ref[0]
    for k in range(1, TOP_K):
        moe = moe + w[:, k:k + 1] * gath_ref[k]
    h2 = h_ref[...] + moe
    ms = jnp.mean(h2 * h2, axis=-1, keepdims=True)
    y_ref[...] = h2 * lax.rsqrt(ms + EPS) * g_ref[...]


def _combine(pos, h, topw, g_final, ys, tt):
    n, d = h.shape
    pos3 = pos.reshape(n // tt, 1, tt * TOP_K)
    return pl.pallas_call(
        functools.partial(_combine_kernel, tt=tt),
        grid=(n // tt,),
        in_specs=[pl.BlockSpec((None, 1, tt * TOP_K), lambda i: (i, 0, 0), memory_space=pltpu.SMEM),
                  pl.BlockSpec((tt, d), lambda i: (i, 0)),
                  pl.BlockSpec((tt, TOP_K), lambda i: (i, 0)),
                  _const_spec((1, d)),
                  pl.BlockSpec(memory_space=pl.ANY)],
        out_specs=pl.BlockSpec((tt, d), lambda i: (i, 0)),
        out_shape=jax.ShapeDtypeStruct((n, d), F32),
        scratch_shapes=[pltpu.VMEM((TOP_K, tt, d), F32), pltpu.SemaphoreType.DMA(())],
        compiler_params=_cparams(1),
        name="moe_combine",
    )(pos3, h, topw, g_final, ys)


def _row_tile(n, want):
    t = min(n, want)
    assert n % t == 0, (n, t)
    return t


MOE_TILE = 256


def _mixer(x3, k_past, v_past, logf_past, conv_past, W, cnt_in):
    b, t, d = x3.shape
    n = b * t
    past = 0 if k_past is None else k_past.shape[1]
    x = x3.reshape(n, d)
    transposed = t % LANES == 0
    tm = _row_tile(t, 256)
    q, k, v, kb, vb, logf, u = _in_proj(x, W["g_mix"], W["wq"], W["wk"], W["wv"], W["wf"], W["wa"], W["wg"],
                                        W["bf"], tm, t, transposed)
    logf3 = logf.reshape(b, t, N_HEADS)
    kb3 = kb.reshape(b, t, D_ATTN)
    if transposed:
        qt, vt = q, vb
        t_q = t
    else:
        t_q = -(-t // LANES) * LANES
        qt = jnp.pad(q.reshape(b, t, D_ATTN), ((0, 0), (0, t_q - t), (0, 0))).transpose(0, 2, 1)
        vt = vb.reshape(b, t, D_ATTN).transpose(0, 2, 1)
    logf_all = logf3
    if past:
        logf_all = jnp.concatenate([logf_past.astype(F32), logf3], axis=1)
        kb3 = jnp.concatenate([k_past.reshape(b, past, D_ATTN).astype(BF16), kb3], axis=1)
        vt = jnp.concatenate([v_past.reshape(b, past, D_ATTN).astype(BF16).transpose(0, 2, 1), vt], axis=2)
        hist = jnp.pad(conv_past, ((0, 0), (HIST_PAD - HIST, 0), (0, 0)))
    else:
        hist = jnp.zeros((b, HIST_PAD, D_CONV), F32)
    t_k = past + t
    if t_k % 1024 == 0 and t_q % 1024 == 0:
        tq = tk = 1024
    elif t_k % LANES == 0 and t_q % LANES == 0 and past == 0:
        tq = tk = LANES
    else:
        tq = t_q
        tk = -(-t_k // LANES) * LANES
        kb3 = jnp.pad(kb3, ((0, 0), (0, tk - t_k), (0, 0)))
        vt = jnp.pad(vt, ((0, 0), (0, 0), (0, tk - t_k)))
        logf_all = jnp.pad(logf_all, ((0, 0), (0, tk - t_k), (0, 0)))
    t_kp = kb3.shape[1]
    aug = _forget_aug(logf_all, 512 if t_kp % 512 == 0 else t_kp)
    attn = _attention(qt, kb3, aug, vt, past, tq, tk)[:, :t]
    u3 = u.reshape(b, t, D_CONV)
    c = _conv(u3, hist, W["w_dw"], W["b_dw"], W["g_ln"], W["b_ln"], _row_tile(t, 256))
    h, hn, topi, topw, rank, cnt = _out_proj(attn.reshape(n, D_ATTN), c.reshape(n, D_CONV), x, W["wo_a"], W["wo_c"],
                                             W["g_ffn"], W["wr"], W["br"], cnt_in, _row_tile(n, 256))
    new_conv = u3[:, t - HIST:] if t >= HIST else jnp.concatenate([hist[:, HIST_PAD - HIST:], u3], axis=1)[:, t:]
    outs = (k.reshape(b, t, N_HEADS, HEAD_DIM), v.reshape(b, t, N_HEADS, HEAD_DIM), logf3, new_conv)
    return (h, hn, topi, topw, rank, cnt), outs


def kernel(x_prompt, x_sample, cache_k, cache_v, cache_logf, cache_conv, g_norm_mix, w_in, b_f, w_dw, b_dw, g_ln,
           b_ln, w_out, g_norm_ffn, w_router, b_router, w_gate, b_gate, w_up, b_up, w_down, b_down, g_norm_final):
    d = x_prompt.shape[-1]
    row = lambda a: a.reshape(1, -1).astype(F32)
    wi = w_in.astype(BF16)
    s0, s1, s2, s3, s4 = D_ATTN, 2 * D_ATTN, 3 * D_ATTN, 3 * D_ATTN + N_HEADS, 3 * D_ATTN + N_HEADS + D_CONV
    wo = w_out.astype(BF16)
    W = {
        "g_mix": row(g_norm_mix),
        "wq": wi[:, :s0], "wk": wi[:, s0:s1], "wv": wi[:, s1:s2],
        "wf": jnp.pad(wi[:, s2:s3], ((0, 0), (0, LANES - N_HEADS))),
        "wa": wi[:, s3:s4], "wg": wi[:, s4:],
        "bf": jnp.pad(row(b_f), ((0, 0), (0, LANES - N_HEADS))),
        "w_dw": jnp.pad(w_dw.astype(F32), ((0, HIST_PAD - CONV_WIDTH), (0, 0))),
        "b_dw": row(b_dw), "g_ln": row(g_ln), "b_ln": row(b_ln),
        "wo_a": wo[:D_ATTN], "wo_c": wo[D_ATTN:],
        "g_ffn": row(g_norm_ffn),
        "wr": jnp.pad(w_router.astype(BF16), ((0, 0), (0, LANES - N_EXPERTS))),
        "br": jnp.pad(row(b_router), ((0, 0), (0, LANES - N_EXPERTS))),
    }
    zero_cnt = jnp.zeros((1, LANES), F32)
    (h_p, hn_p, ti_p, tw_p, rk_p, cnt_p), outs_p = _mixer(x_prompt, None, None, None, None, W, zero_cnt)
    (h_s, hn_s, ti_s, tw_s, rk_s, cnt), outs_s = _mixer(x_sample, cache_k, cache_v, cache_logf, cache_conv, W, cnt_p)

    n_p, n_s = h_p.shape[0], h_s.shape[0]
    tmm = MOE_TILE
    n_tiles = (n_p + n_s) * TOP_K // tmm + N_EXPERTS
    counts = cnt[0, :N_EXPERTS].astype(I32)
    padded = (counts + tmm - 1) // tmm * tmm
    ends = jnp.cumsum(padded)
    offsets = ends - padded
    pos_p = offsets[ti_p] + rk_p
    pos_s = offsets[ti_s] + rk_s
    nt = (ends[-1] // tmm).reshape(1).astype(I32)
    tile_start = jnp.arange(n_tiles, dtype=I32) * tmm
    te = jnp.minimum(jnp.sum((ends[None, :] <= tile_start[:, None]).astype(I32), axis=1), N_EXPERTS - 1)
    pad_start = offsets + counts
    pad_count = padded - counts

    tt = _row_tile(n_s, _row_tile(n_p, 256))
    xs = _dispatch(jnp.concatenate([pos_p, pos_s], axis=0), hn_p, hn_s, pad_start, pad_count, nt, n_tiles, tmm, tt)
    hid = _moe_a(te, nt, xs, w_gate.astype(BF16), w_up.astype(BF16), b_gate[:, None, :], b_up[:, None, :], tmm)
    ys = _moe_b(te, nt, hid, w_down.astype(BF16), b_down[:, None, :], tmm)
    g_fin = row(g_norm_final)
    y_p = _combine(pos_p, h_p, tw_p, g_fin, ys, _row_tile(n_p, 256))
    y_s = _combine(pos_s, h_s, tw_s, g_fin, ys, _row_tile(n_s, 256))
    return (y_p.reshape(x_prompt.shape), y_s.reshape(x_sample.shape)) + outs_p + outs_s
```

```python
import functools

import numpy as np
import jax
import jax.numpy as jnp
from jax import lax
from jax.experimental import pallas as pl
from jax.experimental.pallas import tpu as pltpu

F32 = jnp.float32
BF16 = jnp.bfloat16
I32 = jnp.int32

D_ATTN = 1024
N_HEADS = 16
HEAD_DIM = 64
D_CONV = 1024
CONV_WIDTH = 31
HIST = CONV_WIDTH - 1
N_EXPERTS = 32
TOP_K = 4
SWIGLU_LIMIT = 7.0
SWIGLU_ALPHA = 1.702
EPS = 1e-5
NEG = -1e30
LOG2E = 1.4426950408889634

LANES = 128
MXU_DIM = 256
HIST_PAD = 32
N_SPLIT = 3
VMEM_LIMIT = 56 * 1024 * 1024


def _cparams(n_axes, vmem=VMEM_LIMIT, flags=None):
    return pltpu.CompilerParams(dimension_semantics=("arbitrary",) * n_axes, vmem_limit_bytes=vmem, flags=flags)


def _dot(a, b):
    return jnp.dot(a, b, preferred_element_type=F32)


def _const_spec(shape):
    nd = len(shape)
    return pl.BlockSpec(shape, lambda *_: (0,) * nd, pipeline_mode=pl.Buffered(1))


def _split3(x):
    x1 = x.astype(BF16)
    r1 = x - x1.astype(F32)
    x2 = r1.astype(BF16)
    x3 = (r1 - x2.astype(F32)).astype(BF16)
    return x1, x2, x3


def _in_proj_kernel(x_ref, g_ref, wq_ref, wk_ref, wv_ref, wf_ref, wa_ref, wg_ref, bf_ref,
                    q_ref, k_ref, v_ref, kb_ref, vb_ref, logf_ref, u_ref, *, transposed):
    x = x_ref[...]
    ms = jnp.mean(x * x, axis=-1, keepdims=True)
    xn = (x * lax.rsqrt(ms + EPS) * g_ref[...]).astype(BF16)
    q = _dot(xn, wq_ref[...]) * (HEAD_DIM ** -0.5 * LOG2E)
    q_ref[...] = (q.T if transposed else q).astype(BF16)
    k = _dot(xn, wk_ref[...])
    k_ref[...] = k.reshape(k_ref.shape)
    kb_ref[...] = k.astype(BF16)
    v = _dot(xn, wv_ref[...])
    v_ref[...] = v.reshape(v_ref.shape)
    vb_ref[...] = (v.T if transposed else v).astype(BF16)
    fl = _dot(xn, wf_ref[...]) + bf_ref[...]
    logf = jnp.minimum(fl, 0.0) - jnp.log1p(jnp.exp(-jnp.abs(fl)))
    logf_ref[...] = logf[:, :N_HEADS]
    a = _dot(xn, wa_ref[...])
    g = _dot(xn, wg_ref[...])
    u_ref[...] = a * jax.nn.sigmoid(g)


def _in_proj(x, g_mix, wq, wk, wv, wf, wa, wg, bf_pad, tm, seq_len, transposed):
    n, d = x.shape
    row = lambda w: pl.BlockSpec((tm, w), lambda i: (i, 0))
    if transposed:
        tpb = seq_len // tm
        tspec = pl.BlockSpec((None, D_ATTN, tm), lambda i: (i // tpb, 0, i % tpb))
        tshape = jax.ShapeDtypeStruct((n // seq_len, D_ATTN, seq_len), BF16)
    else:
        tspec, tshape = row(D_ATTN), jax.ShapeDtypeStruct((n, D_ATTN), BF16)
    hspec = pl.BlockSpec((tm, N_HEADS, HEAD_DIM), lambda i: (i, 0, 0))
    return pl.pallas_call(
        functools.partial(_in_proj_kernel, transposed=transposed),
        grid=(n // tm,),
        in_specs=[row(d), _const_spec((1, d)), _const_spec(wq.shape), _const_spec(wk.shape), _const_spec(wv.shape),
                  _const_spec(wf.shape), _const_spec(wa.shape), _const_spec(wg.shape), _const_spec((1, LANES))],
        out_specs=[tspec, hspec, hspec, row(D_ATTN), tspec, row(N_HEADS), row(D_CONV)],
        out_shape=[tshape, jax.ShapeDtypeStruct((n, N_HEADS, HEAD_DIM), F32),
                   jax.ShapeDtypeStruct((n, N_HEADS, HEAD_DIM), F32), jax.ShapeDtypeStruct((n, D_ATTN), BF16),
                   tshape, jax.ShapeDtypeStruct((n, N_HEADS), F32),
                   jax.ShapeDtypeStruct((n, D_CONV), F32)],
        compiler_params=_cparams(1),
        name="in_proj",
    )(x, g_mix, wq, wk, wv, wf, wa, wg, bf_pad)


def _aug_selectors():
    sel = np.zeros((N_SPLIT, N_HEADS, D_ATTN), np.float32)
    for c in range(N_SPLIT):
        for h in range(N_HEADS):
            sel[c, h, (h // 2) * LANES + N_SPLIT * (h % 2) + c] = 1.0
    return jnp.asarray(sel, BF16)


def _forget_aug_kernel(x_ref, sel_ref, o_ref, carry_ref):
    j = pl.program_id(1)

    @pl.when(j == 0)
    def _():
        carry_ref[...] = jnp.zeros_like(carry_ref)

    x = x_ref[...]
    tb = x.shape[0]
    tri = (lax.broadcasted_iota(I32, (tb, tb), 1) <= lax.broadcasted_iota(I32, (tb, tb), 0)).astype(BF16)
    x1, x2, x3 = _split3(x)
    f = _dot(tri, x1) + _dot(tri, x2) + _dot(tri, x3) + carry_ref[...]
    carry_ref[...] = f[tb - 1:tb, :]
    n1, n2, n3 = _split3(-LOG2E * f)
    o_ref[...] = (_dot(n1, sel_ref[0]) + _dot(n2, sel_ref[1]) + _dot(n3, sel_ref[2])).astype(o_ref.dtype)


def _forget_aug(logf, tb):
    b, t, h = logf.shape
    return pl.pallas_call(
        _forget_aug_kernel,
        grid=(b, t // tb),
        in_specs=[pl.BlockSpec((None, tb, h), lambda bi, j: (bi, j, 0)), _const_spec((N_SPLIT, h, D_ATTN))],
        out_specs=pl.BlockSpec((None, tb, D_ATTN), lambda bi, j: (bi, j, 0)),
        out_shape=jax.ShapeDtypeStruct((b, t, D_ATTN), BF16),
        scratch_shapes=[pltpu.VMEM((1, h), F32)],
        compiler_params=_cparams(2),
        name="forget_aug",
    )(logf, _aug_selectors())


def _attn_kernel(si_ref, sj_ref, qt_ref, k_ref, a_ref, vt_ref, o_ref, m_ref, mn_ref, acc_ref, s_ref, p_ref, vth_ref,
                 *, tq, tk, past, nk, cw, aligned):
    step = pl.program_id(2)
    i = si_ref[step]
    j = sj_ref[step]
    q_lo = past + i * tq
    jmax = jnp.minimum((q_lo + tq - 1) // tk, nk - 1)

    @pl.when(j == 0)
    def _():
        m_ref[...] = jnp.full_like(m_ref, NEG)
        acc_ref[...] = jnp.zeros_like(acc_ref)

    unmasked = j * tk + tk - 1 <= q_lo
    row = lax.broadcasted_iota(I32, (LANES, 1), 0)

    def ind(cond):
        return jnp.where(cond, 1.0, 0.0).astype(BF16)

    rb = 64

    def compute(masked):
        row_k = lax.broadcasted_iota(I32, (LANES, tk), 0)
        row_q = lax.broadcasted_iota(I32, (LANES, cw), 0)
        chunks = [(h, c, min(tk, (c + 1) * cw) if (masked and aligned) else tk)
                  for h in range(2) for c in range(tq // cw)]
        q_keep = [ind((row_q >= HEAD_DIM * h) & (row_q < HEAD_DIM * (h + 1))) for h in range(2)]
        sel_rows = [ind((row_q >= N_SPLIT * h) & (row_q < N_SPLIT * (h + 1))) for h in range(2)]
        for h in range(2):
            vth_ref[h] = (vt_ref[...] * ind((row_k >= HEAD_DIM * h) & (row_k < HEAD_DIM * (h + 1)))
                          + ind(row_k == HEAD_DIM * (1 - h)))
        for h, c, rows in chunks:
            cs = pl.ds(c * cw, cw)
            qa = jnp.concatenate([qt_ref[:, cs] * q_keep[h], sel_rows[h]], axis=0)
            kaug = jnp.concatenate([k_ref[0:rows, :], a_ref[0:rows, :]], axis=1)
            s = _dot(kaug, qa)
            if masked:
                kpos = j * tk + lax.broadcasted_iota(I32, (rows, cw), 0)
                qpos = q_lo + c * cw + lax.broadcasted_iota(I32, (rows, cw), 1)
                s = jnp.where(kpos <= qpos, s, NEG)
            s_ref[h, 0:rows, cs] = s
            mn_ref[h, :, cs] = jnp.maximum(m_ref[h, :, cs], jnp.max(s, axis=0, keepdims=True))
        for h, c, rows in chunks:
            cs = pl.ds(c * cw, cw)
            p_ref[h, 0:rows, cs] = jnp.exp2((s_ref[h, 0:rows, cs] - mn_ref[h, :, cs]).astype(BF16))
        for h, c, rows in chunks:
            cs = pl.ds(c * cw, cw)
            alpha = jnp.exp2(m_ref[h, :, cs] - mn_ref[h, :, cs])
            acc_ref[h, :, cs] = alpha * acc_ref[h, :, cs] + _dot(vth_ref[h, :, 0:rows], p_ref[h, 0:rows, cs])
            m_ref[h, :, cs] = mn_ref[h, :, cs]

    @pl.when(unmasked)
    def _():
        compute(False)

    @pl.when(jnp.logical_not(unmasked))
    def _():
        compute(True)

    @pl.when(j == jmax)
    def _():
        a0 = acc_ref[0]
        a1 = acc_ref[1]
        ot = jnp.where(row < HEAD_DIM, a0 / a0[HEAD_DIM:HEAD_DIM + 1, :], a1 / a1[0:1, :])
        o_ref[...] = ot.T.astype(o_ref.dtype)


def _attention(qt, kb, aug, vt, past, tq, tk):
    b, _, t_q = qt.shape
    t_k = kb.shape[1]
    nq, nk = t_q // tq, t_k // tk
    hp = N_HEADS // 2
    steps = [(i, j) for i in range(nq) for j in range(min((past + i * tq + tq - 1) // tk, nk - 1) + 1)]
    si = jnp.asarray(np.array([s[0] for s in steps], np.int32))
    sj = jnp.asarray(np.array([s[1] for s in steps], np.int32))
    cw = min(tq, MXU_DIM)
    return pl.pallas_call(
        functools.partial(_attn_kernel, tq=tq, tk=tk, past=past, nk=nk, cw=cw, aligned=(past == 0 and tq == tk)),
        grid_spec=pltpu.PrefetchScalarGridSpec(
            num_scalar_prefetch=2, grid=(b, hp, len(steps)),
            in_specs=[pl.BlockSpec((None, LANES, tq), lambda bi, p, s, si, sj: (bi, p, si[s])),
                      pl.BlockSpec((None, tk, LANES), lambda bi, p, s, si, sj: (bi, sj[s], p)),
                      pl.BlockSpec((None, tk, LANES), lambda bi, p, s, si, sj: (bi, sj[s], p)),
                      pl.BlockSpec((None, LANES, tk), lambda bi, p, s, si, sj: (bi, p, sj[s]))],
            out_specs=pl.BlockSpec((None, tq, LANES), lambda bi, p, s, si, sj: (bi, si[s], p)),
            scratch_shapes=[pltpu.VMEM((2, 1, tq), F32), pltpu.VMEM((2, 1, tq), F32), pltpu.VMEM((2, LANES, tq), F32),
                            pltpu.VMEM((2, tk, tq), F32), pltpu.VMEM((2, tk, tq), BF16),
                            pltpu.VMEM((2, LANES, tk), BF16)]),
        out_shape=jax.ShapeDtypeStruct((b, t_q, D_ATTN), BF16),
        compiler_params=_cparams(3),
        name="fox_attention",
    )(si, sj, qt, kb, aug, vt)


def _conv_kernel(u_ref, hist_ref, w_ref, bdw_ref, gln_ref, bln_ref, o_ref, win_ref, *, tt):
    t = pl.program_id(1)

    @pl.when(t == 0)
    def _():
        win_ref[0:HIST_PAD, :] = hist_ref[...]

    @pl.when(t > 0)
    def _():
        win_ref[0:HIST_PAD, :] = win_ref[tt:tt + HIST_PAD, :]

    win_ref[HIST_PAD:HIST_PAD + tt, :] = u_ref[...]
    off = HIST_PAD - HIST
    sub = 8
    cols = []
    for c in range(D_CONV // LANES):
        cs = pl.ds(c * LANES, LANES)
        acc = jnp.zeros((tt, LANES), F32) + bdw_ref[:, cs]
        for phase in range(sub):
            taps = [tap for tap in range(CONV_WIDTH) if (off + tap) % sub == phase]
            span = max((off + tap) // sub * sub for tap in taps)
            shifted = win_ref[pl.ds(phase, tt + span), cs]
            for tap in taps:
                lo = (off + tap) // sub * sub
                acc = acc + w_ref[tap:tap + 1, cs] * shifted[lo:lo + tt]
        cols.append(acc)
    y = jnp.concatenate(cols, axis=1)
    mu = jnp.mean(y, axis=-1, keepdims=True)
    yc = y - mu
    var = jnp.mean(yc * yc, axis=-1, keepdims=True)
    z = yc * lax.rsqrt(var + EPS) * gln_ref[...] + bln_ref[...]
    o_ref[...] = (z * jax.nn.sigmoid(z)).astype(o_ref.dtype)


def _conv(u, hist, w_dw, b_dw, g_ln, b_ln, tt):
    b, t, c = u.shape
    return pl.pallas_call(
        functools.partial(_conv_kernel, tt=tt),
        grid=(b, t // tt),
        in_specs=[pl.BlockSpec((None, tt, c), lambda bi, ti: (bi, ti, 0)),
                  pl.BlockSpec((None, HIST_PAD, c), lambda bi, ti: (bi, 0, 0)),
                  _const_spec((HIST_PAD, c)), _const_spec((1, c)), _const_spec((1, c)), _const_spec((1, c))],
        out_specs=pl.BlockSpec((None, tt, c), lambda bi, ti: (bi, ti, 0)),
        out_shape=jax.ShapeDtypeStruct((b, t, c), BF16),
        scratch_shapes=[pltpu.VMEM((HIST_PAD + tt, c), F32)],
        compiler_params=_cparams(2),
        name="conformer_conv",
    )(u, hist, w_dw, b_dw, g_ln, b_ln)


def _out_proj_kernel(attn_ref, c_ref, x_ref, wa_ref, wc_ref, g_ref, wr_ref, br_ref, cnt_in_ref,
                     h_ref, hn_ref, topi_ref, topw_ref, rank_ref, cnt_out_ref, carry_ref):
    i = pl.program_id(0)

    @pl.when(i == 0)
    def _():
        carry_ref[...] = cnt_in_ref[...]

    h = x_ref[...] + _dot(attn_ref[...], wa_ref[...]) + _dot(c_ref[...], wc_ref[...])
    h_ref[...] = h
    ms = jnp.mean(h * h, axis=-1, keepdims=True)
    hn = h * lax.rsqrt(ms + EPS) * g_ref[...]
    hn_ref[...] = hn
    logits = _dot(hn.astype(BF16), wr_ref[...]) + br_ref[...]
    tm = logits.shape[0]
    lane = lax.broadcasted_iota(I32, (tm, LANES), 1)
    lane_f = lane.astype(F32)
    lg = jnp.where(lane < N_EXPERTS, logits, -jnp.inf)
    vals, idxs, hots = [], [], []
    for _ in range(TOP_K):
        mx = jnp.max(lg, axis=1, keepdims=True)
        idx = jnp.min(jnp.where(lg == mx, lane_f, float(LANES)), axis=1, keepdims=True)
        hot = lane_f == idx
        vals.append(mx)
        idxs.append(idx)
        hots.append(hot)
        lg = jnp.where(hot, -jnp.inf, lg)
    es = [jnp.exp(v - vals[0]) for v in vals]
    den = es[0] + es[1] + es[2] + es[3]
    picked = hots[0] | hots[1] | hots[2] | hots[3]
    sel = picked.astype(F32)
    strict = (lax.broadcasted_iota(I32, (tm, tm), 1) < lax.broadcasted_iota(I32, (tm, tm), 0)).astype(BF16)
    before = _dot(strict, sel.astype(BF16)) + carry_ref[...]
    topi = jnp.zeros((tm, LANES), I32)
    topw = jnp.zeros((tm, LANES), F32)
    rank = jnp.zeros((tm, LANES), I32)
    for k in range(TOP_K):
        rk = jnp.sum(jnp.where(hots[k], before, 0.0), axis=1, keepdims=True).astype(I32)
        topi = jnp.where(lane == k, idxs[k].astype(I32), topi)
        topw = jnp.where(lane == k, es[k] / den, topw)
        rank = jnp.where(lane == k, rk, rank)
    topi_ref[...] = topi[:, :TOP_K]
    topw_ref[...] = topw[:, :TOP_K]
    rank_ref[...] = rank[:, :TOP_K]
    carry_ref[...] = carry_ref[...] + jnp.sum(sel, axis=0, keepdims=True)
    cnt_out_ref[...] = carry_ref[...]


def _out_proj(attn, c, x, wa, wc, g_ffn, wr, br, cnt_in, tm):
    n, d = x.shape
    row = lambda w: pl.BlockSpec((tm, w), lambda i: (i, 0))
    return pl.pallas_call(
        _out_proj_kernel,
        grid=(n // tm,),
        in_specs=[row(D_ATTN), row(D_CONV), row(d), _const_spec(wa.shape), _const_spec(wc.shape),
                  _const_spec((1, d)), _const_spec(wr.shape), _const_spec((1, LANES)), _const_spec((1, LANES))],
        out_specs=[row(d), row(d), row(TOP_K), row(TOP_K), row(TOP_K), pl.BlockSpec((1, LANES), lambda i: (0, 0))],
        out_shape=[jax.ShapeDtypeStruct((n, d), F32), jax.ShapeDtypeStruct((n, d), F32),
                   jax.ShapeDtypeStruct((n, TOP_K), I32), jax.ShapeDtypeStruct((n, TOP_K), F32),
                   jax.ShapeDtypeStruct((n, TOP_K), I32), jax.ShapeDtypeStruct((1, LANES), F32)],
        scratch_shapes=[pltpu.VMEM((1, LANES), F32)],
        compiler_params=_cparams(1),
        name="out_proj_router",
    )(attn, c, x, wa, wc, g_ffn, wr, br, cnt_in)


def _row_copy(src, s, dst, d, sem):
    return pltpu.make_async_copy(src.at[pl.ds(s, 1), :], dst.at[pl.ds(d, 1), :], sem)


def _zero_pad_rows(start_ref, count_ref, zero_ref, xs_ref, sem, wait):
    def per_expert(e, carry):
        start = start_ref[e]

        def per_row(r, c2):
            cp = _row_copy(zero_ref, 0, xs_ref, start + r, sem)
            if wait:
                cp.wait()
            else:
                cp.start()
            return c2

        lax.fori_loop(0, count_ref[e], per_row, 0)
        return carry

    lax.fori_loop(0, N_EXPERTS, per_expert, 0)


def _zero_tail_tiles(nt_ref, zero_ref, xs_ref, sem, n_tiles, tile_rows, wait):
    def per_tile(t, carry):
        dst = xs_ref.at[pl.ds(pl.multiple_of(t * tile_rows, tile_rows), tile_rows), :]
        cp = pltpu.make_async_copy(zero_ref, dst, sem)
        if wait:
            cp.wait()
        else:
            cp.start()
        return carry

    lax.fori_loop(nt_ref[0], n_tiles, per_tile, 0)


def _dispatch_kernel(pos_ref, zs_ref, zn_ref, nt_ref, hp_ref, hs_ref, xs_ref, sem, zero_ref,
                     *, tt, np_tiles, n_tiles, tile_rows):
    i = pl.program_id(0)

    @pl.when(i == 0)
    def _():
        zero_ref[...] = jnp.zeros_like(zero_ref)
        _zero_pad_rows(zs_ref, zn_ref, zero_ref, xs_ref, sem, wait=False)
        _zero_tail_tiles(nt_ref, zero_ref, xs_ref, sem, n_tiles, tile_rows, wait=False)
        _zero_pad_rows(zs_ref, zn_ref, zero_ref, xs_ref, sem, wait=True)
        _zero_tail_tiles(nt_ref, zero_ref, xs_ref, sem, n_tiles, tile_rows, wait=True)

    def scatter(hn_ref):
        def issue(r, carry):
            for k in range(TOP_K):
                _row_copy(hn_ref, r, xs_ref, pos_ref[0, r * TOP_K + k], sem).start(priority=k % 2)
            return carry

        lax.fori_loop(0, tt, issue, 0)

        def drain(r, carry):
            for k in range(TOP_K):
                _row_copy(hn_ref, 0, xs_ref, 0, sem).wait()
            return carry

        lax.fori_loop(0, tt, drain, 0)

    @pl.when(i < np_tiles)
    def _():
        scatter(hp_ref)

    @pl.when(i >= np_tiles)
    def _():
        scatter(hs_ref)


def _dispatch(pos, hn_p, hn_s, pad_start, pad_count, nt, n_tiles, tile_rows, tt):
    n_p, d = hn_p.shape
    n_s = hn_s.shape[0]
    np_tiles, ns_tiles = n_p // tt, n_s // tt
    pos3 = pos.reshape(np_tiles + ns_tiles, 1, tt * TOP_K)
    smem = pl.BlockSpec(memory_space=pltpu.SMEM)
    return pl.pallas_call(
        functools.partial(_dispatch_kernel, tt=tt, np_tiles=np_tiles, n_tiles=n_tiles, tile_rows=tile_rows),
        grid=(np_tiles + ns_tiles,),
        in_specs=[pl.BlockSpec((None, 1, tt * TOP_K), lambda i: (i, 0, 0), memory_space=pltpu.SMEM), smem, smem, smem,
                  pl.BlockSpec((tt, d), lambda i: (jnp.minimum(i, np_tiles - 1), 0)),
                  pl.BlockSpec((tt, d), lambda i: (jnp.maximum(i - np_tiles, 0), 0))],
        out_specs=pl.BlockSpec(memory_space=pl.ANY),
        out_shape=jax.ShapeDtypeStruct((n_tiles * tile_rows, d), F32),
        scratch_shapes=[pltpu.SemaphoreType.DMA(()), pltpu.VMEM((tile_rows, d), F32)],
        compiler_params=_cparams(1),
        name="moe_dispatch",
    )(pos3, pad_start, pad_count, nt, hn_p, hn_s)


def _moe_a_kernel(te_ref, nt_ref, x_ref, wg_ref, wu_ref, bg_ref, bu_ref, o_ref, *, cw):
    i = pl.program_id(0)

    @pl.when(i < nt_ref[0])
    def _():
        x = x_ref[...].astype(BF16)
        for c in range(x_ref.shape[1] // cw):
            cs = pl.ds(c * cw, cw)
            gt = jnp.minimum(_dot(x, wg_ref[:, cs]) + bg_ref[:, cs], SWIGLU_LIMIT)
            up = jnp.clip(_dot(x, wu_ref[:, cs]) + bu_ref[:, cs], -SWIGLU_LIMIT, SWIGLU_LIMIT)
            hid = (up + 1.0) * (gt * jax.nn.sigmoid(SWIGLU_ALPHA * gt))
            o_ref[:, cs] = hid.astype(o_ref.dtype)

    @pl.when(i >= nt_ref[0])
    def _():
        o_ref[...] = jnp.zeros_like(o_ref)


def _moe_a(te, nt, xs, wg, wu, bg, bu, tm):
    p, d = xs.shape
    dff = wg.shape[2]
    wspec = pl.BlockSpec((None, d, dff), lambda i, te, nt: (te[i], 0, 0))
    bspec = pl.BlockSpec((None, 1, dff), lambda i, te, nt: (te[i], 0, 0))
    return pl.pallas_call(
        functools.partial(_moe_a_kernel, cw=512),
        grid_spec=pltpu.PrefetchScalarGridSpec(
            num_scalar_prefetch=2, grid=(p // tm,),
            in_specs=[pl.BlockSpec((tm, d), lambda i, te, nt: (i, 0)), wspec, wspec, bspec, bspec],
            out_specs=pl.BlockSpec((tm, dff), lambda i, te, nt: (i, 0))),
        out_shape=jax.ShapeDtypeStruct((p, dff), BF16),
        compiler_params=_cparams(1),
        name="moe_gate_up",
    )(te, nt, xs, wg, wu, bg, bu)


def _moe_b_kernel(te_ref, nt_ref, h_ref, wd_ref, bd_ref, o_ref):
    i = pl.program_id(0)

    @pl.when(i < nt_ref[0])
    def _():
        o_ref[...] = _dot(h_ref[...], wd_ref[...]) + bd_ref[...]

    @pl.when(i >= nt_ref[0])
    def _():
        o_ref[...] = jnp.zeros_like(o_ref)


def _moe_b(te, nt, hid, wd, bd, tm):
    p, dff = hid.shape
    d = wd.shape[2]
    return pl.pallas_call(
        _moe_b_kernel,
        grid_spec=pltpu.PrefetchScalarGridSpec(
            num_scalar_prefetch=2, grid=(p // tm,),
            in_specs=[pl.BlockSpec((tm, dff), lambda i, te, nt: (i, 0)),
                      pl.BlockSpec((None, dff, d), lambda i, te, nt: (te[i], 0, 0)),
                      pl.BlockSpec((None, 1, d), lambda i, te, nt: (te[i], 0, 0))],
            out_specs=pl.BlockSpec((tm, d), lambda i, te, nt: (i, 0))),
        out_shape=jax.ShapeDtypeStruct((p, d), F32),
        compiler_params=_cparams(1),
        name="moe_down",
    )(te, nt, hid, wd, bd)


def _combine_kernel(pos_ref, h_ref, w_ref, g_ref, ys_ref, y_ref, gath_ref, sem, *, tt):
    def issue(r, carry):
        for k in range(TOP_K):
            _row_copy(ys_ref, pos_ref[0, r * TOP_K + k], gath_ref.at[k], r, sem).start(priority=k % 2)
        return carry

    lax.fori_loop(0, tt, issue, 0)

    def drain(r, carry):
        for k in range(TOP_K):
            _row_copy(ys_ref, 0, gath_ref.at[k], 0, sem).wait()
        return carry

    lax.fori_loop(0, tt, drain, 0)
    w = w_ref[...]
    moe = w[:, 0:1] * gath_ref[0]
    for k in range(1, TOP_K):
        moe = moe + w[:, k:k + 1] * gath_ref[k]
    h2 = h_ref[...] + moe
    ms = jnp.mean(h2 * h2, axis=-1, keepdims=True)
    y_ref[...] = h2 * lax.rsqrt(ms + EPS) * g_ref[...]


def _combine(pos, h, topw, g_final, ys, tt):
    n, d = h.shape
    pos3 = pos.reshape(n // tt, 1, tt * TOP_K)
    return pl.pallas_call(
        functools.partial(_combine_kernel, tt=tt),
        grid=(n // tt,),
        in_specs=[pl.BlockSpec((None, 1, tt * TOP_K), lambda i: (i, 0, 0), memory_space=pltpu.SMEM),
                  pl.BlockSpec((tt, d), lambda i: (i, 0)),
                  pl.BlockSpec((tt, TOP_K), lambda i: (i, 0)),
                  _const_spec((1, d)),
                  pl.BlockSpec(memory_space=pl.ANY)],
        out_specs=pl.BlockSpec((tt, d), lambda i: (i, 0)),
        out_shape=jax.ShapeDtypeStruct((n, d), F32),
        scratch_shapes=[pltpu.VMEM((TOP_K, tt, d), F32), pltpu.SemaphoreType.DMA(())],
        compiler_params=_cparams(1),
        name="moe_combine",
    )(pos3, h, topw, g_final, ys)


def _row_tile(n, want):
    t = min(n, want)
    assert n % t == 0, (n, t)
    return t


MOE_TILE = 256


def _mixer(x3, k_past, v_past, logf_past, conv_past, W, cnt_in):
    b, t, d = x3.shape
    n = b * t
    past = 0 if k_past is None else k_past.shape[1]
    x = x3.reshape(n, d)
    transposed = t % LANES == 0
    tm = _row_tile(t, 256)
    q, k, v, kb, vb, logf, u = _in_proj(x, W["g_mix"], W["wq"], W["wk"], W["wv"], W["wf"], W["wa"], W["wg"],
                                        W["bf"], tm, t, transposed)
    logf3 = logf.reshape(b, t, N_HEADS)
    kb3 = kb.reshape(b, t, D_ATTN)
    if transposed:
        qt, vt = q, vb
        t_q = t
    else:
        t_q = -(-t // LANES) * LANES
        qt = jnp.pad(q.reshape(b, t, D_ATTN), ((0, 0), (0, t_q - t), (0, 0))).transpose(0, 2, 1)
        vt = vb.reshape(b, t, D_ATTN).transpose(0, 2, 1)
    logf_all = logf3
    if past:
        logf_all = jnp.concatenate([logf_past.astype(F32), logf3], axis=1)
        kb3 = jnp.concatenate([k_past.reshape(b, past, D_ATTN).astype(BF16), kb3], axis=1)
        vt = jnp.concatenate([v_past.reshape(b, past, D_ATTN).astype(BF16).transpose(0, 2, 1), vt], axis=2)
        hist = jnp.pad(conv_past, ((0, 0), (HIST_PAD - HIST, 0), (0, 0)))
    else:
        hist = jnp.zeros((b, HIST_PAD, D_CONV), F32)
    t_k = past + t
    if t_k % 1024 == 0 and t_q % 1024 == 0:
        tq = tk = 1024
    elif t_k % LANES == 0 and t_q % LANES == 0 and past == 0:
        tq = tk = LANES
    else:
        tq = t_q
        tk = -(-t_k // LANES) * LANES
        kb3 = jnp.pad(kb3, ((0, 0), (0, tk - t_k), (0, 0)))
        vt = jnp.pad(vt, ((0, 0), (0, 0), (0, tk - t_k)))
        logf_all = jnp.pad(logf_all, ((0, 0), (0, tk - t_k), (0, 0)))
    t_kp = kb3.shape[1]
    aug = _forget_aug(logf_all, 512 if t_kp % 512 == 0 else t_kp)
    attn = _attention(qt, kb3, aug, vt, past, tq, tk)[:, :t]
    u3 = u.reshape(b, t, D_CONV)
    c = _conv(u3, hist, W["w_dw"], W["b_dw"], W["g_ln"], W["b_ln"], _row_tile(t, 256))
    h, hn, topi, topw, rank, cnt = _out_proj(attn.reshape(n, D_ATTN), c.reshape(n, D_CONV), x, W["wo_a"], W["wo_c"],
                                             W["g_ffn"], W["wr"], W["br"], cnt_in, _row_tile(n, 256))
    new_conv = u3[:, t - HIST:] if t >= HIST else jnp.concatenate([hist[:, HIST_PAD - HIST:], u3], axis=1)[:, t:]
    outs = (k.reshape(b, t, N_HEADS, HEAD_DIM), v.reshape(b, t, N_HEADS, HEAD_DIM), logf3, new_conv)
    return (h, hn, topi, topw, rank, cnt), outs


def kernel(x_prompt, x_sample, cache_k, cache_v, cache_logf, cache_conv, g_norm_mix, w_in, b_f, w_dw, b_dw, g_ln,
           b_ln, w_out, g_norm_ffn, w_router, b_router, w_gate, b_gate, w_up, b_up, w_down, b_down, g_norm_final):
    d = x_prompt.shape[-1]
    row = lambda a: a.reshape(1, -1).astype(F32)
    wi = w_in.astype(BF16)
    s0, s1, s2, s3, s4 = D_ATTN, 2 * D_ATTN, 3 * D_ATTN, 3 * D_ATTN + N_HEADS, 3 * D_ATTN + N_HEADS + D_CONV
    wo = w_out.astype(BF16)
    W = {
        "g_mix": row(g_norm_mix),
        "wq": wi[:, :s0], "wk": wi[:, s0:s1], "wv": wi[:, s1:s2],
        "wf": jnp.pad(wi[:, s2:s3], ((0, 0), (0, LANES - N_HEADS))),
        "wa": wi[:, s3:s4], "wg": wi[:, s4:],
        "bf": jnp.pad(row(b_f), ((0, 0), (0, LANES - N_HEADS))),
        "w_dw": jnp.pad(w_dw.astype(F32), ((0, HIST_PAD - CONV_WIDTH), (0, 0))),
        "b_dw": row(b_dw), "g_ln": row(g_ln), "b_ln": row(b_ln),
        "wo_a": wo[:D_ATTN], "wo_c": wo[D_ATTN:],
        "g_ffn": row(g_norm_ffn),
        "wr": jnp.pad(w_router.astype(BF16), ((0, 0), (0, LANES - N_EXPERTS))),
        "br": jnp.pad(row(b_router), ((0, 0), (0, LANES - N_EXPERTS))),
    }
    zero_cnt = jnp.zeros((1, LANES), F32)
    (h_p, hn_p, ti_p, tw_p, rk_p, cnt_p), outs_p = _mixer(x_prompt, None, None, None, None, W, zero_cnt)
    (h_s, hn_s, ti_s, tw_s, rk_s, cnt), outs_s = _mixer(x_sample, cache_k, cache_v, cache_logf, cache_conv, W, cnt_p)

    n_p, n_s = h_p.shape[0], h_s.shape[0]
    tmm = MOE_TILE
    n_tiles = (n_p + n_s) * TOP_K // tmm + N_EXPERTS
    counts = cnt[0, :N_EXPERTS].astype(I32)
    padded = (counts + tmm - 1) // tmm * tmm
    ends = jnp.cumsum(padded)
    offsets = ends - padded
    pos_p = offsets[ti_p] + rk_p
    pos_s = offsets[ti_s] + rk_s
    nt = (ends[-1] // tmm).reshape(1).astype(I32)
    tile_start = jnp.arange(n_tiles, dtype=I32) * tmm
    te = jnp.minimum(jnp.sum((ends[None, :] <= tile_start[:, None]).astype(I32), axis=1), N_EXPERTS - 1)
    pad_start = offsets + counts
    pad_count = padded - counts

    tt = _row_tile(n_s, _row_tile(n_p, 256))
    xs = _dispatch(jnp.concatenate([pos_p, pos_s], axis=0), hn_p, hn_s, pad_start, pad_count, nt, n_tiles, tmm, tt)
    hid = _moe_a(te, nt, xs, w_gate.astype(BF16), w_up.astype(BF16), b_gate[:, None, :], b_up[:, None, :], tmm)
    ys = _moe_b(te, nt, hid, w_down.astype(BF16), b_down[:, None, :], tmm)
    g_fin = row(g_norm_final)
    y_p = _combine(pos_p, h_p, tw_p, g_fin, ys, _row_tile(n_p, 256))
    y_s = _combine(pos_s, h_s, tw_s, g_fin, ys, _row_tile(n_s, 256))
    return (y_p.reshape(x_prompt.shape), y_s.reshape(x_sample.shape)) + outs_p + outs_s
```

```python
import functools

import numpy as np
import jax
import jax.numpy as jnp
from jax import lax
from jax.experimental import pallas as pl
from jax.experimental.pallas import tpu as pltpu

F32 = jnp.float32
BF16 = jnp.bfloat16
I32 = jnp.int32

D_ATTN = 1024
N_HEADS = 16
HEAD_DIM = 64
D_CONV = 1024
CONV_WIDTH = 31
HIST = CONV_WIDTH - 1
N_EXPERTS = 32
TOP_K = 4
SWIGLU_LIMIT = 7.0
SWIGLU_ALPHA = 1.702
EPS = 1e-5
NEG = -1e30

LANES = 128
MXU_DIM = 256
HIST_PAD = 32
N_SPLIT = 3
VMEM_LIMIT = 56 * 1024 * 1024


def _cparams(n_axes, vmem=VMEM_LIMIT):
    return pltpu.CompilerParams(dimension_semantics=("arbitrary",) * n_axes, vmem_limit_bytes=vmem)


def _dot(a, b):
    return jnp.dot(a, b, preferred_element_type=F32)


def _const_spec(shape):
    nd = len(shape)
    return pl.BlockSpec(shape, lambda *_: (0,) * nd, pipeline_mode=pl.Buffered(1))


def _split3(x):
    x1 = x.astype(BF16)
    r1 = x - x1.astype(F32)
    x2 = r1.astype(BF16)
    x3 = (r1 - x2.astype(F32)).astype(BF16)
    return x1, x2, x3


def _in_proj_kernel(x_ref, g_ref, wq_ref, wk_ref, wv_ref, wf_ref, wa_ref, wg_ref, bf_ref,
                    q_ref, k_ref, v_ref, kb_ref, vb_ref, logf_ref, u_ref, *, transposed):
    x = x_ref[...]
    ms = jnp.mean(x * x, axis=-1, keepdims=True)
    xn = (x * lax.rsqrt(ms + EPS) * g_ref[...]).astype(BF16)
    q = _dot(xn, wq_ref[...]) * (HEAD_DIM ** -0.5)
    q_ref[...] = (q.T if transposed else q).astype(BF16)
    k = _dot(xn, wk_ref[...])
    k_ref[...] = k.reshape(k_ref.shape)
    kb_ref[...] = k.astype(BF16)
    v = _dot(xn, wv_ref[...])
    v_ref[...] = v.reshape(v_ref.shape)
    vb_ref[...] = (v.T if transposed else v).astype(BF16)
    fl = _dot(xn, wf_ref[...]) + bf_ref[...]
    logf = jnp.minimum(fl, 0.0) - jnp.log1p(jnp.exp(-jnp.abs(fl)))
    logf_ref[...] = logf[:, :N_HEADS]
    a = _dot(xn, wa_ref[...])
    g = _dot(xn, wg_ref[...])
    u_ref[...] = a * jax.nn.sigmoid(g)


def _in_proj(x, g_mix, wq, wk, wv, wf, wa, wg, bf_pad, tm, seq_len, transposed):
    n, d = x.shape
    row = lambda w: pl.BlockSpec((tm, w), lambda i: (i, 0))
    if transposed:
        tpb = seq_len // tm
        tspec = pl.BlockSpec((None, D_ATTN, tm), lambda i: (i // tpb, 0, i % tpb))
        tshape = jax.ShapeDtypeStruct((n // seq_len, D_ATTN, seq_len), BF16)
    else:
        tspec, tshape = row(D_ATTN), jax.ShapeDtypeStruct((n, D_ATTN), BF16)
    hspec = pl.BlockSpec((tm, N_HEADS, HEAD_DIM), lambda i: (i, 0, 0))
    return pl.pallas_call(
        functools.partial(_in_proj_kernel, transposed=transposed),
        grid=(n // tm,),
        in_specs=[row(d), _const_spec((1, d)), _const_spec(wq.shape), _const_spec(wk.shape), _const_spec(wv.shape),
                  _const_spec(wf.shape), _const_spec(wa.shape), _const_spec(wg.shape), _const_spec((1, LANES))],
        out_specs=[tspec, hspec, hspec, row(D_ATTN), tspec, row(N_HEADS), row(D_CONV)],
        out_shape=[tshape, jax.ShapeDtypeStruct((n, N_HEADS, HEAD_DIM), F32),
                   jax.ShapeDtypeStruct((n, N_HEADS, HEAD_DIM), F32), jax.ShapeDtypeStruct((n, D_ATTN), BF16),
                   tshape, jax.ShapeDtypeStruct((n, N_HEADS), F32),
                   jax.ShapeDtypeStruct((n, D_CONV), F32)],
        compiler_params=_cparams(1),
        name="in_proj",
    )(x, g_mix, wq, wk, wv, wf, wa, wg, bf_pad)


def _aug_selectors():
    sel = np.zeros((N_SPLIT, N_HEADS, D_ATTN), np.float32)
    for c in range(N_SPLIT):
        for h in range(N_HEADS):
            sel[c, h, (h // 2) * LANES + N_SPLIT * (h % 2) + c] = 1.0
    return jnp.asarray(sel, BF16)


def _forget_aug_kernel(x_ref, sel_ref, o_ref, carry_ref):
    j = pl.program_id(1)

    @pl.when(j == 0)
    def _():
        carry_ref[...] = jnp.zeros_like(carry_ref)

    x = x_ref[...]
    tb = x.shape[0]
    tri = (lax.broadcasted_iota(I32, (tb, tb), 1) <= lax.broadcasted_iota(I32, (tb, tb), 0)).astype(BF16)
    x1, x2, x3 = _split3(x)
    f = _dot(tri, x1) + _dot(tri, x2) + _dot(tri, x3) + carry_ref[...]
    carry_ref[...] = f[tb - 1:tb, :]
    n1, n2, n3 = _split3(-f)
    o_ref[...] = (_dot(n1, sel_ref[0]) + _dot(n2, sel_ref[1]) + _dot(n3, sel_ref[2])).astype(o_ref.dtype)


def _forget_aug(logf, tb):
    b, t, h = logf.shape
    return pl.pallas_call(
        _forget_aug_kernel,
        grid=(b, t // tb),
        in_specs=[pl.BlockSpec((None, tb, h), lambda bi, j: (bi, j, 0)), _const_spec((N_SPLIT, h, D_ATTN))],
        out_specs=pl.BlockSpec((None, tb, D_ATTN), lambda bi, j: (bi, j, 0)),
        out_shape=jax.ShapeDtypeStruct((b, t, D_ATTN), BF16),
        scratch_shapes=[pltpu.VMEM((1, h), F32)],
        compiler_params=_cparams(2),
        name="forget_aug",
    )(logf, _aug_selectors())


def _attn_kernel(si_ref, sj_ref, qt_ref, k_ref, a_ref, vt_ref, o_ref, m_ref, mn_ref, acc_ref, s_ref, p_ref,
                 *, tq, tk, past, nk, cw, aligned):
    step = pl.program_id(2)
    i = si_ref[step]
    j = sj_ref[step]
    q_lo = past + i * tq
    jmax = jnp.minimum((q_lo + tq - 1) // tk, nk - 1)

    @pl.when(j == 0)
    def _():
        m_ref[...] = jnp.full_like(m_ref, NEG)
        acc_ref[...] = jnp.zeros_like(acc_ref)

    unmasked = j * tk + tk - 1 <= q_lo
    row = lax.broadcasted_iota(I32, (LANES, 1), 0)

    def ind(cond):
        return jnp.where(cond, 1.0, 0.0).astype(BF16)

    rb = 64

    def compute(masked):
        kaug = jnp.concatenate([k_ref[...], a_ref[...]], axis=1)
        vt = vt_ref[...]
        row_k = lax.broadcasted_iota(I32, (LANES, tk), 0)
        row_q = lax.broadcasted_iota(I32, (LANES, cw), 0)
        chunks = [(h, c, min(tk, (c + 1) * cw) if (masked and aligned) else tk)
                  for h in range(2) for c in range(tq // cw)]
        q_keep = [ind((row_q >= HEAD_DIM * h) & (row_q < HEAD_DIM * (h + 1))) for h in range(2)]
        sel_rows = [ind((row_q >= N_SPLIT * h) & (row_q < N_SPLIT * (h + 1))) for h in range(2)]
        vth = [vt * ind((row_k >= HEAD_DIM * h) & (row_k < HEAD_DIM * (h + 1))) + ind(row_k == HEAD_DIM * (1 - h))
               for h in range(2)]
        for h, c, rows in chunks:
            cs = pl.ds(c * cw, cw)
            qa = jnp.concatenate([qt_ref[:, cs] * q_keep[h], sel_rows[h]], axis=0)
            s = _dot(kaug[:rows], qa)
            if masked:
                kpos = j * tk + lax.broadcasted_iota(I32, (rows, cw), 0)
                qpos = q_lo + c * cw + lax.broadcasted_iota(I32, (rows, cw), 1)
                s = jnp.where(kpos <= qpos, s, NEG)
            s_ref[h, 0:rows, cs] = s
            mn_ref[h, :, cs] = jnp.maximum(m_ref[h, :, cs], jnp.max(s, axis=0, keepdims=True))
        for h, c, rows in chunks:
            cs = pl.ds(c * cw, cw)
            p_ref[h, 0:rows, cs] = jnp.exp(s_ref[h, 0:rows, cs] - mn_ref[h, :, cs]).astype(BF16)
        for h, c, rows in chunks:
            cs = pl.ds(c * cw, cw)
            alpha = jnp.exp(m_ref[h, :, cs] - mn_ref[h, :, cs])
            acc_ref[h, :, cs] = alpha * acc_ref[h, :, cs] + _dot(vth[h][:, :rows], p_ref[h, 0:rows, cs])
            m_ref[h, :, cs] = mn_ref[h, :, cs]

    @pl.when(unmasked)
    def _():
        compute(False)

    @pl.when(jnp.logical_not(unmasked))
    def _():
        compute(True)

    @pl.when(j == jmax)
    def _():
        a0 = acc_ref[0]
        a1 = acc_ref[1]
        ot = jnp.where(row < HEAD_DIM, a0 / a0[HEAD_DIM:HEAD_DIM + 1, :], a1 / a1[0:1, :])
        o_ref[...] = ot.T.astype(o_ref.dtype)


def _attention(qt, kb, aug, vt, past, tq, tk):
    b, _, t_q = qt.shape
    t_k = kb.shape[1]
    nq, nk = t_q // tq, t_k // tk
    hp = N_HEADS // 2
    steps = [(i, j) for i in range(nq) for j in range(min((past + i * tq + tq - 1) // tk, nk - 1) + 1)]
    si = jnp.asarray(np.array([s[0] for s in steps], np.int32))
    sj = jnp.asarray(np.array([s[1] for s in steps], np.int32))
    cw = min(tq, MXU_DIM)
    return pl.pallas_call(
        functools.partial(_attn_kernel, tq=tq, tk=tk, past=past, nk=nk, cw=cw, aligned=(past == 0 and tq == tk)),
        grid_spec=pltpu.PrefetchScalarGridSpec(
            num_scalar_prefetch=2, grid=(b, hp, len(steps)),
            in_specs=[pl.BlockSpec((None, LANES, tq), lambda bi, p, s, si, sj: (bi, p, si[s])),
                      pl.BlockSpec((None, tk, LANES), lambda bi, p, s, si, sj: (bi, sj[s], p)),
                      pl.BlockSpec((None, tk, LANES), lambda bi, p, s, si, sj: (bi, sj[s], p)),
                      pl.BlockSpec((None, LANES, tk), lambda bi, p, s, si, sj: (bi, p, sj[s]))],
            out_specs=pl.BlockSpec((None, tq, LANES), lambda bi, p, s, si, sj: (bi, si[s], p)),
            scratch_shapes=[pltpu.VMEM((2, 1, tq), F32), pltpu.VMEM((2, 1, tq), F32), pltpu.VMEM((2, LANES, tq), F32),
                            pltpu.VMEM((2, tk, tq), F32), pltpu.VMEM((2, tk, tq), BF16)]),
        out_shape=jax.ShapeDtypeStruct((b, t_q, D_ATTN), BF16),
        compiler_params=_cparams(3),
        name="fox_attention",
    )(si, sj, qt, kb, aug, vt)


def _conv_kernel(u_ref, hist_ref, w_ref, bdw_ref, gln_ref, bln_ref, o_ref, win_ref, *, tt):
    t = pl.program_id(1)

    @pl.when(t == 0)
    def _():
        win_ref[0:HIST_PAD, :] = hist_ref[...]

    @pl.when(t > 0)
    def _():
        win_ref[0:HIST_PAD, :] = win_ref[tt:tt + HIST_PAD, :]

    win_ref[HIST_PAD:HIST_PAD + tt, :] = u_ref[...]
    off = HIST_PAD - HIST
    sub = 8
    cols = []
    for c in range(D_CONV // LANES):
        cs = pl.ds(c * LANES, LANES)
        acc = jnp.zeros((tt, LANES), F32) + bdw_ref[:, cs]
        for phase in range(sub):
            taps = [tap for tap in range(CONV_WIDTH) if (off + tap) % sub == phase]
            span = max((off + tap) // sub * sub for tap in taps)
            shifted = win_ref[pl.ds(phase, tt + span), cs]
            for tap in taps:
                lo = (off + tap) // sub * sub
                acc = acc + w_ref[tap:tap + 1, cs] * shifted[lo:lo + tt]
        cols.append(acc)
    y = jnp.concatenate(cols, axis=1)
    mu = jnp.mean(y, axis=-1, keepdims=True)
    yc = y - mu
    var = jnp.mean(yc * yc, axis=-1, keepdims=True)
    z = yc * lax.rsqrt(var + EPS) * gln_ref[...] + bln_ref[...]
    o_ref[...] = (z * jax.nn.sigmoid(z)).astype(o_ref.dtype)


def _conv(u, hist, w_dw, b_dw, g_ln, b_ln, tt):
    b, t, c = u.shape
    return pl.pallas_call(
        functools.partial(_conv_kernel, tt=tt),
        grid=(b, t // tt),
        in_specs=[pl.BlockSpec((None, tt, c), lambda bi, ti: (bi, ti, 0)),
                  pl.BlockSpec((None, HIST_PAD, c), lambda bi, ti: (bi, 0, 0)),
                  _const_spec((HIST_PAD, c)), _const_spec((1, c)), _const_spec((1, c)), _const_spec((1, c))],
        out_specs=pl.BlockSpec((None, tt, c), lambda bi, ti: (bi, ti, 0)),
        out_shape=jax.ShapeDtypeStruct((b, t, c), BF16),
        scratch_shapes=[pltpu.VMEM((HIST_PAD + tt, c), F32)],
        compiler_params=_cparams(2),
        name="conformer_conv",
    )(u, hist, w_dw, b_dw, g_ln, b_ln)


def _out_proj_kernel(attn_ref, c_ref, x_ref, wa_ref, wc_ref, g_ref, wr_ref, br_ref, cnt_in_ref,
                     h_ref, hn_ref, topi_ref, topw_ref, rank_ref, cnt_out_ref, carry_ref):
    i = pl.program_id(0)

    @pl.when(i == 0)
    def _():
        carry_ref[...] = cnt_in_ref[...]

    h = x_ref[...] + _dot(attn_ref[...], wa_ref[...]) + _dot(c_ref[...], wc_ref[...])
    h_ref[...] = h
    ms = jnp.mean(h * h, axis=-1, keepdims=True)
    hn = h * lax.rsqrt(ms + EPS) * g_ref[...]
    hn_ref[...] = hn.reshape(hn_ref.shape)
    logits = _dot(hn.astype(BF16), wr_ref[...]) + br_ref[...]
    tm = logits.shape[0]
    lane = lax.broadcasted_iota(I32, (tm, LANES), 1)
    lane_f = lane.astype(F32)
    lg = jnp.where(lane < N_EXPERTS, logits, -jnp.inf)
    vals, idxs, hots = [], [], []
    for _ in range(TOP_K):
        mx = jnp.max(lg, axis=1, keepdims=True)
        idx = jnp.min(jnp.where(lg == mx, lane_f, float(LANES)), axis=1, keepdims=True)
        hot = lane_f == idx
        vals.append(mx)
        idxs.append(idx)
        hots.append(hot)
        lg = jnp.where(hot, -jnp.inf, lg)
    es = [jnp.exp(v - vals[0]) for v in vals]
    den = es[0] + es[1] + es[2] + es[3]
    picked = hots[0] | hots[1] | hots[2] | hots[3]
    sel = picked.astype(F32)
    strict = (lax.broadcasted_iota(I32, (tm, tm), 1) < lax.broadcasted_iota(I32, (tm, tm), 0)).astype(BF16)
    before = _dot(strict, sel.astype(BF16)) + carry_ref[...]
    topi = jnp.zeros((tm, LANES), I32)
    topw = jnp.zeros((tm, LANES), F32)
    rank = jnp.zeros((tm, LANES), I32)
    for k in range(TOP_K):
        rk = jnp.sum(jnp.where(hots[k], before, 0.0), axis=1, keepdims=True).astype(I32)
        topi = jnp.where(lane == k, idxs[k].astype(I32), topi)
        topw = jnp.where(lane == k, es[k] / den, topw)
        rank = jnp.where(lane == k, rk, rank)
    topi_ref[...] = topi[:, :TOP_K]
    topw_ref[...] = topw[:, :TOP_K]
    rank_ref[...] = rank[:, :TOP_K]
    carry_ref[...] = carry_ref[...] + jnp.sum(sel, axis=0, keepdims=True)
    cnt_out_ref[...] = carry_ref[...]


def _out_proj(attn, c, x, wa, wc, g_ffn, wr, br, cnt_in, tm):
    n, d = x.shape
    row = lambda w: pl.BlockSpec((tm, w), lambda i: (i, 0))
    return pl.pallas_call(
        _out_proj_kernel,
        grid=(n // tm,),
        in_specs=[row(D_ATTN), row(D_CONV), row(d), _const_spec(wa.shape), _const_spec(wc.shape),
                  _const_spec((1, d)), _const_spec(wr.shape), _const_spec((1, LANES)), _const_spec((1, LANES))],
        out_specs=[row(d), _slab_spec(tm, d, lambda i: (i, 0, 0)), row(TOP_K), row(TOP_K), row(TOP_K),
                   pl.BlockSpec((1, LANES), lambda i: (0, 0))],
        out_shape=[jax.ShapeDtypeStruct((n, d), F32), jax.ShapeDtypeStruct((n,) + _slab(d), F32),
                   jax.ShapeDtypeStruct((n, TOP_K), I32), jax.ShapeDtypeStruct((n, TOP_K), F32),
                   jax.ShapeDtypeStruct((n, TOP_K), I32), jax.ShapeDtypeStruct((1, LANES), F32)],
        scratch_shapes=[pltpu.VMEM((1, LANES), F32)],
        compiler_params=_cparams(1),
        name="out_proj_router",
    )(attn, c, x, wa, wc, g_ffn, wr, br, cnt_in)


def _slab(d):
    return (d // LANES, LANES)


def _slab_spec(rows, d, index_map):
    return pl.BlockSpec((rows,) + _slab(d), index_map)


def _row_copy(src, s, dst, d, sem):
    return pltpu.make_async_copy(src.at[pl.ds(s, 1)], dst.at[pl.ds(d, 1)], sem)


def _zero_pad_rows(start_ref, count_ref, zero_ref, xs_ref, sem, wait):
    def per_expert(e, carry):
        start = start_ref[e]

        def per_row(r, c2):
            cp = _row_copy(zero_ref, 0, xs_ref, start + r, sem)
            if wait:
                cp.wait()
            else:
                cp.start()
            return c2

        lax.fori_loop(0, count_ref[e], per_row, 0)
        return carry

    lax.fori_loop(0, N_EXPERTS, per_expert, 0)


def _zero_tail_tiles(nt_ref, zero_ref, xs_ref, sem, n_tiles, tile_rows, wait):
    def per_tile(t, carry):
        dst = xs_ref.at[pl.ds(pl.multiple_of(t * tile_rows, tile_rows), tile_rows)]
        cp = pltpu.make_async_copy(zero_ref, dst, sem)
        if wait:
            cp.wait()
        else:
            cp.start()
        return carry

    lax.fori_loop(nt_ref[0], n_tiles, per_tile, 0)


def _dispatch_kernel(pos_ref, zs_ref, zn_ref, nt_ref, hp_ref, hs_ref, xs_ref, sem, zero_ref,
                     *, tt, np_tiles, n_tiles, tile_rows):
    i = pl.program_id(0)

    @pl.when(i == 0)
    def _():
        zero_ref[...] = jnp.zeros_like(zero_ref)
        _zero_pad_rows(zs_ref, zn_ref, zero_ref, xs_ref, sem, wait=False)
        _zero_tail_tiles(nt_ref, zero_ref, xs_ref, sem, n_tiles, tile_rows, wait=False)
        _zero_pad_rows(zs_ref, zn_ref, zero_ref, xs_ref, sem, wait=True)
        _zero_tail_tiles(nt_ref, zero_ref, xs_ref, sem, n_tiles, tile_rows, wait=True)

    def scatter(hn_ref):
        def issue(r, carry):
            for k in range(TOP_K):
                _row_copy(hn_ref, r, xs_ref, pos_ref[0, r * TOP_K + k], sem).start(priority=k % 2)
            return carry

        lax.fori_loop(0, tt, issue, 0)

        def drain(r, carry):
            for k in range(TOP_K):
                _row_copy(hn_ref, 0, xs_ref, 0, sem).wait()
            return carry

        lax.fori_loop(0, tt, drain, 0)

    @pl.when(i < np_tiles)
    def _():
        scatter(hp_ref)

    @pl.when(i >= np_tiles)
    def _():
        scatter(hs_ref)


def _dispatch(pos, hn_p, hn_s, pad_start, pad_count, nt, n_tiles, tile_rows, tt):
    n_p, n_s = hn_p.shape[0], hn_s.shape[0]
    slab = hn_p.shape[1:]
    np_tiles, ns_tiles = n_p // tt, n_s // tt
    pos3 = pos.reshape(np_tiles + ns_tiles, 1, tt * TOP_K)
    smem = pl.BlockSpec(memory_space=pltpu.SMEM)
    return pl.pallas_call(
        functools.partial(_dispatch_kernel, tt=tt, np_tiles=np_tiles, n_tiles=n_tiles, tile_rows=tile_rows),
        grid=(np_tiles + ns_tiles,),
        in_specs=[pl.BlockSpec((None, 1, tt * TOP_K), lambda i: (i, 0, 0), memory_space=pltpu.SMEM), smem, smem, smem,
                  pl.BlockSpec((tt,) + slab, lambda i: (jnp.minimum(i, np_tiles - 1), 0, 0)),
                  pl.BlockSpec((tt,) + slab, lambda i: (jnp.maximum(i - np_tiles, 0), 0, 0))],
        out_specs=pl.BlockSpec(memory_space=pl.ANY),
        out_shape=jax.ShapeDtypeStruct((n_tiles * tile_rows,) + slab, F32),
        scratch_shapes=[pltpu.SemaphoreType.DMA(()), pltpu.VMEM((tile_rows,) + slab, F32)],
        compiler_params=_cparams(1),
        name="moe_dispatch",
    )(pos3, pad_start, pad_count, nt, hn_p, hn_s)


def _moe_a_kernel(te_ref, nt_ref, x_ref, wg_ref, wu_ref, bg_ref, bu_ref, o_ref, *, cw):
    i = pl.program_id(0)

    @pl.when(i < nt_ref[0])
    def _():
        x = x_ref[...].reshape(x_ref.shape[0], wg_ref.shape[0]).astype(BF16)
        for c in range(wg_ref.shape[1] // cw):
            cs = pl.ds(c * cw, cw)
            gt = jnp.minimum(_dot(x, wg_ref[:, cs]) + bg_ref[:, cs], SWIGLU_LIMIT)
            up = jnp.clip(_dot(x, wu_ref[:, cs]) + bu_ref[:, cs], -SWIGLU_LIMIT, SWIGLU_LIMIT)
            hid = (up + 1.0) * (gt * jax.nn.sigmoid(SWIGLU_ALPHA * gt))
            o_ref[:, cs] = hid.astype(o_ref.dtype)

    @pl.when(i >= nt_ref[0])
    def _():
        o_ref[...] = jnp.zeros_like(o_ref)


def _moe_a(te, nt, xs, wg, wu, bg, bu, tm):
    p = xs.shape[0]
    d, dff = wg.shape[1:]
    wspec = pl.BlockSpec((None, d, dff), lambda i, te, nt: (te[i], 0, 0))
    bspec = pl.BlockSpec((None, 1, dff), lambda i, te, nt: (te[i], 0, 0))
    return pl.pallas_call(
        functools.partial(_moe_a_kernel, cw=512),
        grid_spec=pltpu.PrefetchScalarGridSpec(
            num_scalar_prefetch=2, grid=(p // tm,),
            in_specs=[_slab_spec(tm, d, lambda i, te, nt: (i, 0, 0)), wspec, wspec, bspec, bspec],
            out_specs=pl.BlockSpec((tm, dff), lambda i, te, nt: (i, 0))),
        out_shape=jax.ShapeDtypeStruct((p, dff), BF16),
        compiler_params=_cparams(1),
        name="moe_gate_up",
    )(te, nt, xs, wg, wu, bg, bu)


def _moe_b_kernel(te_ref, nt_ref, h_ref, wd_ref, bd_ref, o_ref):
    i = pl.program_id(0)

    @pl.when(i < nt_ref[0])
    def _():
        o_ref[...] = (_dot(h_ref[...], wd_ref[...]) + bd_ref[...]).reshape(o_ref.shape)

    @pl.when(i >= nt_ref[0])
    def _():
        o_ref[...] = jnp.zeros_like(o_ref)


def _moe_b(te, nt, hid, wd, bd, tm):
    p, dff = hid.shape
    d = wd.shape[2]
    return pl.pallas_call(
        _moe_b_kernel,
        grid_spec=pltpu.PrefetchScalarGridSpec(
            num_scalar_prefetch=2, grid=(p // tm,),
            in_specs=[pl.BlockSpec((tm, dff), lambda i, te, nt: (i, 0)),
                      pl.BlockSpec((None, dff, d), lambda i, te, nt: (te[i], 0, 0)),
                      pl.BlockSpec((None, 1, d), lambda i, te, nt: (te[i], 0, 0))],
            out_specs=_slab_spec(tm, d, lambda i, te, nt: (i, 0, 0))),
        out_shape=jax.ShapeDtypeStruct((p,) + _slab(d), F32),
        compiler_params=_cparams(1),
        name="moe_down",
    )(te, nt, hid, wd, bd)


def _combine_kernel(pos_ref, h_ref, w_ref, g_ref, ys_ref, y_ref, gath_ref, sem, *, tt):
    def issue(r, carry):
        for k in range(TOP_K):
            _row_copy(ys_ref, pos_ref[0, r * TOP_K + k], gath_ref.at[k], r, sem).start(priority=k % 2)
        return carry

    lax.fori_loop(0, tt, issue, 0)

    def drain(r, carry):
        for k in range(TOP_K):
            _row_copy(ys_ref, 0, gath_ref.at[k], 0, sem).wait()
        return carry

    lax.fori_loop(0, tt, drain, 0)
    w = w_ref[...]
    rows = lambda k: gath_ref[k].reshape(h_ref.shape)
    moe = w[:, 0:1] * rows(0)
    for k in range(1, TOP_K):
        moe = moe + w[:, k:k + 1] * rows(k)
    h2 = h_ref[...] + moe
    ms = jnp.mean(h2 * h2, axis=-1, keepdims=True)
    y_ref[...] = h2 * lax.rsqrt(ms + EPS) * g_ref[...]


def _combine(pos, h, topw, g_final, ys, tt):
    n, d = h.shape
    pos3 = pos.reshape(n // tt, 1, tt * TOP_K)
    return pl.pallas_call(
        functools.partial(_combine_kernel, tt=tt),
        grid=(n // tt,),
        in_specs=[pl.BlockSpec((None, 1, tt * TOP_K), lambda i: (i, 0, 0), memory_space=pltpu.SMEM),
                  pl.BlockSpec((tt, d), lambda i: (i, 0)),
                  pl.BlockSpec((tt, TOP_K), lambda i: (i, 0)),
                  _const_spec((1, d)),
                  pl.BlockSpec(memory_space=pl.ANY)],
        out_specs=pl.BlockSpec((tt, d), lambda i: (i, 0)),
        out_shape=jax.ShapeDtypeStruct((n, d), F32),
        scratch_shapes=[pltpu.VMEM((TOP_K, tt) + _slab(d), F32), pltpu.SemaphoreType.DMA(())],
        compiler_params=_cparams(1),
        name="moe_combine",
    )(pos3, h, topw, g_final, ys)


def _row_tile(n, want):
    t = min(n, want)
    assert n % t == 0, (n, t)
    return t


MOE_TILE = 256


def _mixer(x3, k_past, v_past, logf_past, conv_past, W, cnt_in):
    b, t, d = x3.shape
    n = b * t
    past = 0 if k_past is None else k_past.shape[1]
    x = x3.reshape(n, d)
    transposed = t % LANES == 0
    tm = _row_tile(t, 256)
    q, k, v, kb, vb, logf, u = _in_proj(x, W["g_mix"], W["wq"], W["wk"], W["wv"], W["wf"], W["wa"], W["wg"],
                                        W["bf"], tm, t, transposed)
    logf3 = logf.reshape(b, t, N_HEADS)
    kb3 = kb.reshape(b, t, D_ATTN)
    if transposed:
        qt, vt = q, vb
        t_q = t
    else:
        t_q = -(-t // LANES) * LANES
        qt = jnp.pad(q.reshape(b, t, D_ATTN), ((0, 0), (0, t_q - t), (0, 0))).transpose(0, 2, 1)
        vt = vb.reshape(b, t, D_ATTN).transpose(0, 2, 1)
    logf_all = logf3
    if past:
        logf_all = jnp.concatenate([logf_past.astype(F32), logf3], axis=1)
        kb3 = jnp.concatenate([k_past.reshape(b, past, D_ATTN).astype(BF16), kb3], axis=1)
        vt = jnp.concatenate([v_past.reshape(b, past, D_ATTN).astype(BF16).transpose(0, 2, 1), vt], axis=2)
        hist = jnp.pad(conv_past, ((0, 0), (HIST_PAD - HIST, 0), (0, 0)))
    else:
        hist = jnp.zeros((b, HIST_PAD, D_CONV), F32)
    t_k = past + t
    if t_k % 1024 == 0 and t_q % 1024 == 0:
        tq = tk = 1024
    elif t_k % LANES == 0 and t_q % LANES == 0 and past == 0:
        tq = tk = LANES
    else:
        tq = t_q
        tk = -(-t_k // LANES) * LANES
        kb3 = jnp.pad(kb3, ((0, 0), (0, tk - t_k), (0, 0)))
        vt = jnp.pad(vt, ((0, 0), (0, 0), (0, tk - t_k)))
        logf_all = jnp.pad(logf_all, ((0, 0), (0, tk - t_k), (0, 0)))
    t_kp = kb3.shape[1]
    aug = _forget_aug(logf_all, 512 if t_kp % 512 == 0 else t_kp)
    attn = _attention(qt, kb3, aug, vt, past, tq, tk)[:, :t]
    u3 = u.reshape(b, t, D_CONV)
    c = _conv(u3, hist, W["w_dw"], W["b_dw"], W["g_ln"], W["b_ln"], _row_tile(t, 256))
    h, hn, topi, topw, rank, cnt = _out_proj(attn.reshape(n, D_ATTN), c.reshape(n, D_CONV), x, W["wo_a"], W["wo_c"],
                                             W["g_ffn"], W["wr"], W["br"], cnt_in, _row_tile(n, 256))
    new_conv = u3[:, t - HIST:] if t >= HIST else jnp.concatenate([hist[:, HIST_PAD - HIST:], u3], axis=1)[:, t:]
    outs = (k.reshape(b, t, N_HEADS, HEAD_DIM), v.reshape(b, t, N_HEADS, HEAD_DIM), logf3, new_conv)
    return (h, hn, topi, topw, rank, cnt), outs


def kernel(x_prompt, x_sample, cache_k, cache_v, cache_logf, cache_conv, g_norm_mix, w_in, b_f, w_dw, b_dw, g_ln,
           b_ln, w_out, g_norm_ffn, w_router, b_router, w_gate, b_gate, w_up, b_up, w_down, b_down, g_norm_final):
    d = x_prompt.shape[-1]
    row = lambda a: a.reshape(1, -1).astype(F32)
    wi = w_in.astype(BF16)
    s0, s1, s2, s3, s4 = D_ATTN, 2 * D_ATTN, 3 * D_ATTN, 3 * D_ATTN + N_HEADS, 3 * D_ATTN + N_HEADS + D_CONV
    wo = w_out.astype(BF16)
    W = {
        "g_mix": row(g_norm_mix),
        "wq": wi[:, :s0], "wk": wi[:, s0:s1], "wv": wi[:, s1:s2],
        "wf": jnp.pad(wi[:, s2:s3], ((0, 0), (0, LANES - N_HEADS))),
        "wa": wi[:, s3:s4], "wg": wi[:, s4:],
        "bf": jnp.pad(row(b_f), ((0, 0), (0, LANES - N_HEADS))),
        "w_dw": jnp.pad(w_dw.astype(F32), ((0, HIST_PAD - CONV_WIDTH), (0, 0))),
        "b_dw": row(b_dw), "g_ln": row(g_ln), "b_ln": row(b_ln),
        "wo_a": wo[:D_ATTN], "wo_c": wo[D_ATTN:],
        "g_ffn": row(g_norm_ffn),
        "wr": jnp.pad(w_router.astype(BF16), ((0, 0), (0, LANES - N_EXPERTS))),
        "br": jnp.pad(row(b_router), ((0, 0), (0, LANES - N_EXPERTS))),
    }
    zero_cnt = jnp.zeros((1, LANES), F32)
    (h_p, hn_p, ti_p, tw_p, rk_p, cnt_p), outs_p = _mixer(x_prompt, None, None, None, None, W, zero_cnt)
    (h_s, hn_s, ti_s, tw_s, rk_s, cnt), outs_s = _mixer(x_sample, cache_k, cache_v, cache_logf, cache_conv, W, cnt_p)

    n_p, n_s = h_p.shape[0], h_s.shape[0]
    tmm = MOE_TILE
    n_tiles = (n_p + n_s) * TOP_K // tmm + N_EXPERTS
    counts = cnt[0, :N_EXPERTS].astype(I32)
    padded = (counts + tmm - 1) // tmm * tmm
    ends = jnp.cumsum(padded)
    offsets = ends - padded
    pos_p = offsets[ti_p] + rk_p
    pos_s = offsets[ti_s] + rk_s
    nt = (ends[-1] // tmm).reshape(1).astype(I32)
    tile_start = jnp.arange(n_tiles, dtype=I32) * tmm
    te = jnp.minimum(jnp.sum((ends[None, :] <= tile_start[:, None]).astype(I32), axis=1), N_EXPERTS - 1)
    pad_start = offsets + counts
    pad_count = padded - counts

    tt = _row_tile(n_s, _row_tile(n_p, 256))
    xs = _dispatch(jnp.concatenate([pos_p, pos_s], axis=0), hn_p, hn_s, pad_start, pad_count, nt, n_tiles, tmm, tt)
    hid = _moe_a(te, nt, xs, w_gate.astype(BF16), w_up.astype(BF16), b_gate[:, None, :], b_up[:, None, :], tmm)
    ys = _moe_b(te, nt, hid, w_down.astype(BF16), b_down[:, None, :], tmm)
    g_fin = row(g_norm_final)
    y_p = _combine(pos_p, h_p, tw_p, g_fin, ys, _row_tile(n_p, 256))
    y_s = _combine(pos_s, h_s, tw_s, g_fin, ys, _row_tile(n_s, 256))
    return (y_p.reshape(x_prompt.shape), y_s.reshape(x_sample.shape)) + outs_p + outs_s
```

```python
import functools

import numpy as np
import jax
import jax.numpy as jnp
from jax import lax
from jax.experimental import pallas as pl
from jax.experimental.pallas import tpu as pltpu

F32 = jnp.float32
BF16 = jnp.bfloat16
I32 = jnp.int32

D_ATTN = 1024
N_HEADS = 16
HEAD_DIM = 64
D_CONV = 1024
CONV_WIDTH = 31
HIST = CONV_WIDTH - 1
N_EXPERTS = 32
TOP_K = 4
SWIGLU_LIMIT = 7.0
SWIGLU_ALPHA = 1.702
EPS = 1e-5
NEG = -1e30

LANES = 128
MXU_DIM = 256
HIST_PAD = 32
N_SPLIT = 3
VMEM_LIMIT = 56 * 1024 * 1024


def _cparams(n_axes, vmem=VMEM_LIMIT):
    return pltpu.CompilerParams(dimension_semantics=("arbitrary",) * n_axes, vmem_limit_bytes=vmem)


def _dot(a, b):
    return jnp.dot(a, b, preferred_element_type=F32)


def _const_spec(shape):
    nd = len(shape)
    return pl.BlockSpec(shape, lambda *_: (0,) * nd, pipeline_mode=pl.Buffered(1))


def _split3(x):
    x1 = x.astype(BF16)
    r1 = x - x1.astype(F32)
    x2 = r1.astype(BF16)
    x3 = (r1 - x2.astype(F32)).astype(BF16)
    return x1, x2, x3


def _in_proj_kernel(x_ref, g_ref, wq_ref, wk_ref, wv_ref, wf_ref, wa_ref, wg_ref, bf_ref,
                    q_ref, k_ref, v_ref, kb_ref, vb_ref, logf_ref, u_ref, *, transposed):
    x = x_ref[...]
    ms = jnp.mean(x * x, axis=-1, keepdims=True)
    xn = (x * lax.rsqrt(ms + EPS) * g_ref[...]).astype(BF16)
    q = _dot(xn, wq_ref[...]) * (HEAD_DIM ** -0.5)
    q_ref[...] = (q.T if transposed else q).astype(BF16)
    k = _dot(xn, wk_ref[...])
    k_ref[...] = k.reshape(k_ref.shape)
    kb_ref[...] = k.astype(BF16)
    v = _dot(xn, wv_ref[...])
    v_ref[...] = v.reshape(v_ref.shape)
    vb_ref[...] = (v.T if transposed else v).astype(BF16)
    fl = _dot(xn, wf_ref[...]) + bf_ref[...]
    logf = jnp.minimum(fl, 0.0) - jnp.log1p(jnp.exp(-jnp.abs(fl)))
    logf_ref[...] = logf[:, :N_HEADS]
    a = _dot(xn, wa_ref[...])
    g = _dot(xn, wg_ref[...])
    u_ref[...] = a * jax.nn.sigmoid(g)


def _in_proj(x, g_mix, wq, wk, wv, wf, wa, wg, bf_pad, tm, seq_len, transposed):
    n, d = x.shape
    row = lambda w: pl.BlockSpec((tm, w), lambda i: (i, 0))
    if transposed:
        tpb = seq_len // tm
        tspec = pl.BlockSpec((None, D_ATTN, tm), lambda i: (i // tpb, 0, i % tpb))
        tshape = jax.ShapeDtypeStruct((n // seq_len, D_ATTN, seq_len), BF16)
    else:
        tspec, tshape = row(D_ATTN), jax.ShapeDtypeStruct((n, D_ATTN), BF16)
    hspec = pl.BlockSpec((tm, N_HEADS, HEAD_DIM), lambda i: (i, 0, 0))
    return pl.pallas_call(
        functools.partial(_in_proj_kernel, transposed=transposed),
        grid=(n // tm,),
        in_specs=[row(d), _const_spec((1, d)), _const_spec(wq.shape), _const_spec(wk.shape), _const_spec(wv.shape),
                  _const_spec(wf.shape), _const_spec(wa.shape), _const_spec(wg.shape), _const_spec((1, LANES))],
        out_specs=[tspec, hspec, hspec, row(D_ATTN), tspec, row(N_HEADS), row(D_CONV)],
        out_shape=[tshape, jax.ShapeDtypeStruct((n, N_HEADS, HEAD_DIM), F32),
                   jax.ShapeDtypeStruct((n, N_HEADS, HEAD_DIM), F32), jax.ShapeDtypeStruct((n, D_ATTN), BF16),
                   tshape, jax.ShapeDtypeStruct((n, N_HEADS), F32),
                   jax.ShapeDtypeStruct((n, D_CONV), F32)],
        compiler_params=_cparams(1),
        name="in_proj",
    )(x, g_mix, wq, wk, wv, wf, wa, wg, bf_pad)


def _aug_selectors():
    sel = np.zeros((N_SPLIT, N_HEADS, D_ATTN), np.float32)
    for c in range(N_SPLIT):
        for h in range(N_HEADS):
            sel[c, h, (h // 2) * LANES + N_SPLIT * (h % 2) + c] = 1.0
    return jnp.asarray(sel, BF16)


def _forget_aug_kernel(x_ref, sel_ref, o_ref, carry_ref):
    j = pl.program_id(1)

    @pl.when(j == 0)
    def _():
        carry_ref[...] = jnp.zeros_like(carry_ref)

    x = x_ref[...]
    tb = x.shape[0]
    tri = (lax.broadcasted_iota(I32, (tb, tb), 1) <= lax.broadcasted_iota(I32, (tb, tb), 0)).astype(BF16)
    x1, x2, x3 = _split3(x)
    f = _dot(tri, x1) + _dot(tri, x2) + _dot(tri, x3) + carry_ref[...]
    carry_ref[...] = f[tb - 1:tb, :]
    n1, n2, n3 = _split3(-f)
    o_ref[...] = (_dot(n1, sel_ref[0]) + _dot(n2, sel_ref[1]) + _dot(n3, sel_ref[2])).astype(o_ref.dtype)


def _forget_aug(logf, tb):
    b, t, h = logf.shape
    return pl.pallas_call(
        _forget_aug_kernel,
        grid=(b, t // tb),
        in_specs=[pl.BlockSpec((None, tb, h), lambda bi, j: (bi, j, 0)), _const_spec((N_SPLIT, h, D_ATTN))],
        out_specs=pl.BlockSpec((None, tb, D_ATTN), lambda bi, j: (bi, j, 0)),
        out_shape=jax.ShapeDtypeStruct((b, t, D_ATTN), BF16),
        scratch_shapes=[pltpu.VMEM((1, h), F32)],
        compiler_params=_cparams(2),
        name="forget_aug",
    )(logf, _aug_selectors())


def _attn_kernel(si_ref, sj_ref, qt_ref, k_ref, a_ref, vt_ref, o_ref, m_ref, mn_ref, acc_ref, s_ref, p_ref,
                 *, tq, tk, past, nk, cw, aligned):
    step = pl.program_id(2)
    i = si_ref[step]
    j = sj_ref[step]
    q_lo = past + i * tq
    jmax = jnp.minimum((q_lo + tq - 1) // tk, nk - 1)

    @pl.when(j == 0)
    def _():
        m_ref[...] = jnp.full_like(m_ref, NEG)
        acc_ref[...] = jnp.zeros_like(acc_ref)

    unmasked = j * tk + tk - 1 <= q_lo
    row = lax.broadcasted_iota(I32, (LANES, 1), 0)

    def ind(cond):
        return jnp.where(cond, 1.0, 0.0).astype(BF16)

    rb = 64

    def compute(masked):
        kaug = jnp.concatenate([k_ref[...], a_ref[...]], axis=1)
        vt = vt_ref[...]
        row_k = lax.broadcasted_iota(I32, (LANES, tk), 0)
        row_q = lax.broadcasted_iota(I32, (LANES, cw), 0)
        chunks = [(h, c, min(tk, (c + 1) * cw) if (masked and aligned) else tk)
                  for h in range(2) for c in range(tq // cw)]
        q_keep = [ind((row_q >= HEAD_DIM * h) & (row_q < HEAD_DIM * (h + 1))) for h in range(2)]
        sel_rows = [ind((row_q >= N_SPLIT * h) & (row_q < N_SPLIT * (h + 1))) for h in range(2)]
        vth = [vt * ind((row_k >= HEAD_DIM * h) & (row_k < HEAD_DIM * (h + 1))) + ind(row_k == HEAD_DIM * (1 - h))
               for h in range(2)]
        for h, c, rows in chunks:
            cs = pl.ds(c * cw, cw)
            qa = jnp.concatenate([qt_ref[:, cs] * q_keep[h], sel_rows[h]], axis=0)
            s = _dot(kaug[:rows], qa)
            if masked:
                kpos = j * tk + lax.broadcasted_iota(I32, (rows, cw), 0)
                qpos = q_lo + c * cw + lax.broadcasted_iota(I32, (rows, cw), 1)
                s = jnp.where(kpos <= qpos, s, NEG)
            s_ref[h, 0:rows, cs] = s
            mn_ref[h, :, cs] = jnp.maximum(m_ref[h, :, cs], jnp.max(s, axis=0, keepdims=True))
        for h, c, rows in chunks:
            cs = pl.ds(c * cw, cw)
            p_ref[h, 0:rows, cs] = jnp.exp(s_ref[h, 0:rows, cs] - mn_ref[h, :, cs]).astype(BF16)
        for h, c, rows in chunks:
            cs = pl.ds(c * cw, cw)
            alpha = jnp.exp(m_ref[h, :, cs] - mn_ref[h, :, cs])
            acc_ref[h, :, cs] = alpha * acc_ref[h, :, cs] + _dot(vth[h][:, :rows], p_ref[h, 0:rows, cs])
            m_ref[h, :, cs] = mn_ref[h, :, cs]

    @pl.when(unmasked)
    def _():
        compute(False)

    @pl.when(jnp.logical_not(unmasked))
    def _():
        compute(True)

    @pl.when(j == jmax)
    def _():
        a0 = acc_ref[0]
        a1 = acc_ref[1]
        ot = jnp.where(row < HEAD_DIM, a0 / a0[HEAD_DIM:HEAD_DIM + 1, :], a1 / a1[0:1, :])
        o_ref[...] = ot.T.astype(o_ref.dtype)


def _attention(qt, kb, aug, vt, past, tq, tk):
    b, _, t_q = qt.shape
    t_k = kb.shape[1]
    nq, nk = t_q // tq, t_k // tk
    hp = N_HEADS // 2
    steps = [(i, j) for i in range(nq) for j in range(min((past + i * tq + tq - 1) // tk, nk - 1) + 1)]
    si = jnp.asarray(np.array([s[0] for s in steps], np.int32))
    sj = jnp.asarray(np.array([s[1] for s in steps], np.int32))
    cw = min(tq, MXU_DIM)
    return pl.pallas_call(
        functools.partial(_attn_kernel, tq=tq, tk=tk, past=past, nk=nk, cw=cw, aligned=(past == 0 and tq == tk)),
        grid_spec=pltpu.PrefetchScalarGridSpec(
            num_scalar_prefetch=2, grid=(b, hp, len(steps)),
            in_specs=[pl.BlockSpec((None, LANES, tq), lambda bi, p, s, si, sj: (bi, p, si[s])),
                      pl.BlockSpec((None, tk, LANES), lambda bi, p, s, si, sj: (bi, sj[s], p)),
                      pl.BlockSpec((None, tk, LANES), lambda bi, p, s, si, sj: (bi, sj[s], p)),
                      pl.BlockSpec((None, LANES, tk), lambda bi, p, s, si, sj: (bi, p, sj[s]))],
            out_specs=pl.BlockSpec((None, tq, LANES), lambda bi, p, s, si, sj: (bi, si[s], p)),
            scratch_shapes=[pltpu.VMEM((2, 1, tq), F32), pltpu.VMEM((2, 1, tq), F32), pltpu.VMEM((2, LANES, tq), F32),
                            pltpu.VMEM((2, tk, tq), F32), pltpu.VMEM((2, tk, tq), BF16)]),
        out_shape=jax.ShapeDtypeStruct((b, t_q, D_ATTN), BF16),
        compiler_params=_cparams(3),
        name="fox_attention",
    )(si, sj, qt, kb, aug, vt)


def _conv_kernel(u_ref, hist_ref, w_ref, bdw_ref, gln_ref, bln_ref, o_ref, win_ref, *, tt):
    t = pl.program_id(1)

    @pl.when(t == 0)
    def _():
        win_ref[0:HIST_PAD, :] = hist_ref[...]

    @pl.when(t > 0)
    def _():
        win_ref[0:HIST_PAD, :] = win_ref[tt:tt + HIST_PAD, :]

    win_ref[HIST_PAD:HIST_PAD + tt, :] = u_ref[...]
    off = HIST_PAD - HIST
    sub = 8
    cols = []
    for c in range(D_CONV // LANES):
        cs = pl.ds(c * LANES, LANES)
        acc = jnp.zeros((tt, LANES), F32) + bdw_ref[:, cs]
        for phase in range(sub):
            taps = [tap for tap in range(CONV_WIDTH) if (off + tap) % sub == phase]
            span = max((off + tap) // sub * sub for tap in taps)
            shifted = win_ref[pl.ds(phase, tt + span), cs]
            for tap in taps:
                lo = (off + tap) // sub * sub
                acc = acc + w_ref[tap:tap + 1, cs] * shifted[lo:lo + tt]
        cols.append(acc)
    y = jnp.concatenate(cols, axis=1)
    mu = jnp.mean(y, axis=-1, keepdims=True)
    yc = y - mu
    var = jnp.mean(yc * yc, axis=-1, keepdims=True)
    z = yc * lax.rsqrt(var + EPS) * gln_ref[...] + bln_ref[...]
    o_ref[...] = (z * jax.nn.sigmoid(z)).astype(o_ref.dtype)


def _conv(u, hist, w_dw, b_dw, g_ln, b_ln, tt):
    b, t, c = u.shape
    return pl.pallas_call(
        functools.partial(_conv_kernel, tt=tt),
        grid=(b, t // tt),
        in_specs=[pl.BlockSpec((None, tt, c), lambda bi, ti: (bi, ti, 0)),
                  pl.BlockSpec((None, HIST_PAD, c), lambda bi, ti: (bi, 0, 0)),
                  _const_spec((HIST_PAD, c)), _const_spec((1, c)), _const_spec((1, c)), _const_spec((1, c))],
        out_specs=pl.BlockSpec((None, tt, c), lambda bi, ti: (bi, ti, 0)),
        out_shape=jax.ShapeDtypeStruct((b, t, c), BF16),
        scratch_shapes=[pltpu.VMEM((HIST_PAD + tt, c), F32)],
        compiler_params=_cparams(2),
        name="conformer_conv",
    )(u, hist, w_dw, b_dw, g_ln, b_ln)


def _out_proj_kernel(attn_ref, c_ref, x_ref, wa_ref, wc_ref, g_ref, wr_ref, br_ref, cnt_in_ref,
                     h_ref, hn_ref, topi_ref, topw_ref, rank_ref, cnt_out_ref, carry_ref):
    i = pl.program_id(0)

    @pl.when(i == 0)
    def _():
        carry_ref[...] = cnt_in_ref[...]

    h = x_ref[...] + _dot(attn_ref[...], wa_ref[...]) + _dot(c_ref[...], wc_ref[...])
    h_ref[...] = h
    ms = jnp.mean(h * h, axis=-1, keepdims=True)
    hn = h * lax.rsqrt(ms + EPS) * g_ref[...]
    hn_ref[...] = hn.reshape(hn_ref.shape)
    logits = _dot(hn.astype(BF16), wr_ref[...]) + br_ref[...]
    tm = logits.shape[0]
    lane = lax.broadcasted_iota(I32, (tm, LANES), 1)
    lane_f = lane.astype(F32)
    lg = jnp.where(lane < N_EXPERTS, logits, -jnp.inf)
    vals, idxs, hots = [], [], []
    for _ in range(TOP_K):
        mx = jnp.max(lg, axis=1, keepdims=True)
        idx = jnp.min(jnp.where(lg == mx, lane_f, float(LANES)), axis=1, keepdims=True)
        hot = lane_f == idx
        vals.append(mx)
        idxs.append(idx)
        hots.append(hot)
        lg = jnp.where(hot, -jnp.inf, lg)
    es = [jnp.exp(v - vals[0]) for v in vals]
    den = es[0] + es[1] + es[2] + es[3]
    picked = hots[0] | hots[1] | hots[2] | hots[3]
    sel = picked.astype(F32)
    strict = (lax.broadcasted_iota(I32, (tm, tm), 1) < lax.broadcasted_iota(I32, (tm, tm), 0)).astype(BF16)
    before = _dot(strict, sel.astype(BF16)) + carry_ref[...]
    topi = jnp.zeros((tm, LANES), I32)
    topw = jnp.zeros((tm, LANES), F32)
    rank = jnp.zeros((tm, LANES), I32)
    for k in range(TOP_K):
        rk = jnp.sum(jnp.where(hots[k], before, 0.0), axis=1, keepdims=True).astype(I32)
        topi = jnp.where(lane == k, idxs[k].astype(I32), topi)
        topw = jnp.where(lane == k, es[k] / den, topw)
        rank = jnp.where(lane == k, rk, rank)
    topi_ref[...] = topi[:, :TOP_K]
    topw_ref[...] = topw[:, :TOP_K]
    rank_ref[...] = rank[:, :TOP_K]
    carry_ref[...] = carry_ref[...] + jnp.sum(sel, axis=0, keepdims=True)
    cnt_out_ref[...] = carry_ref[...]


def _out_proj(attn, c, x, wa, wc, g_ffn, wr, br, cnt_in, tm):
    n, d = x.shape
    row = lambda w: pl.BlockSpec((tm, w), lambda i: (i, 0))
    return pl.pallas_call(
        _out_proj_kernel,
        grid=(n // tm,),
        in_specs=[row(D_ATTN), row(D_CONV), row(d), _const_spec(wa.shape), _const_spec(wc.shape),
                  _const_spec((1, d)), _const_spec(wr.shape), _const_spec((1, LANES)), _const_spec((1, LANES))],
        out_specs=[row(d), _slab_spec(tm, d, lambda i: (i, 0, 0)), row(TOP_K), row(TOP_K), row(TOP_K),
                   pl.BlockSpec((1, LANES), lambda i: (0, 0))],
        out_shape=[jax.ShapeDtypeStruct((n, d), F32), jax.ShapeDtypeStruct((n,) + _slab(d), F32),
                   jax.ShapeDtypeStruct((n, TOP_K), I32), jax.ShapeDtypeStruct((n, TOP_K), F32),
                   jax.ShapeDtypeStruct((n, TOP_K), I32), jax.ShapeDtypeStruct((1, LANES), F32)],
        scratch_shapes=[pltpu.VMEM((1, LANES), F32)],
        compiler_params=_cparams(1),
        name="out_proj_router",
    )(attn, c, x, wa, wc, g_ffn, wr, br, cnt_in)


def _slab(d):
    return (d // LANES, LANES)


def _slab_spec(rows, d, index_map):
    return pl.BlockSpec((rows,) + _slab(d), index_map)


def _row_copy(src, s, dst, d, sem):
    return pltpu.make_async_copy(src.at[pl.ds(s, 1)], dst.at[pl.ds(d, 1)], sem)


def _zero_pad_rows(start_ref, count_ref, zero_ref, xs_ref, sem, wait):
    def per_expert(e, carry):
        start = start_ref[e]

        def per_row(r, c2):
            cp = _row_copy(zero_ref, 0, xs_ref, start + r, sem)
            if wait:
                cp.wait()
            else:
                cp.start()
            return c2

        lax.fori_loop(0, count_ref[e], per_row, 0)
        return carry

    lax.fori_loop(0, N_EXPERTS, per_expert, 0)


def _zero_tail_tiles(nt_ref, zero_ref, xs_ref, sem, n_tiles, tile_rows, wait):
    def per_tile(t, carry):
        dst = xs_ref.at[pl.ds(pl.multiple_of(t * tile_rows, tile_rows), tile_rows)]
        cp = pltpu.make_async_copy(zero_ref, dst, sem)
        if wait:
            cp.wait()
        else:
            cp.start()
        return carry

    lax.fori_loop(nt_ref[0], n_tiles, per_tile, 0)


def _dispatch_kernel(pos_ref, zs_ref, zn_ref, nt_ref, hp_ref, hs_ref, xs_ref, sem, zero_ref,
                     *, tt, np_tiles, n_tiles, tile_rows):
    i = pl.program_id(0)

    @pl.when(i == 0)
    def _():
        zero_ref[...] = jnp.zeros_like(zero_ref)
        _zero_pad_rows(zs_ref, zn_ref, zero_ref, xs_ref, sem, wait=False)
        _zero_tail_tiles(nt_ref, zero_ref, xs_ref, sem, n_tiles, tile_rows, wait=False)
        _zero_pad_rows(zs_ref, zn_ref, zero_ref, xs_ref, sem, wait=True)
        _zero_tail_tiles(nt_ref, zero_ref, xs_ref, sem, n_tiles, tile_rows, wait=True)

    def scatter(hn_ref):
        def issue(r, carry):
            for k in range(TOP_K):
                _row_copy(hn_ref, r, xs_ref, pos_ref[0, r * TOP_K + k], sem).start(priority=k % 2)
            return carry

        lax.fori_loop(0, tt, issue, 0)

        def drain(r, carry):
            for k in range(TOP_K):
                _row_copy(hn_ref, 0, xs_ref, 0, sem).wait()
            return carry

        lax.fori_loop(0, tt, drain, 0)

    @pl.when(i < np_tiles)
    def _():
        scatter(hp_ref)

    @pl.when(i >= np_tiles)
    def _():
        scatter(hs_ref)


def _dispatch(pos, hn_p, hn_s, pad_start, pad_count, nt, n_tiles, tile_rows, tt):
    n_p, n_s = hn_p.shape[0], hn_s.shape[0]
    slab = hn_p.shape[1:]
    np_tiles, ns_tiles = n_p // tt, n_s // tt
    pos3 = pos.reshape(np_tiles + ns_tiles, 1, tt * TOP_K)
    smem = pl.BlockSpec(memory_space=pltpu.SMEM)
    return pl.pallas_call(
        functools.partial(_dispatch_kernel, tt=tt, np_tiles=np_tiles, n_tiles=n_tiles, tile_rows=tile_rows),
        grid=(np_tiles + ns_tiles,),
        in_specs=[pl.BlockSpec((None, 1, tt * TOP_K), lambda i: (i, 0, 0), memory_space=pltpu.SMEM), smem, smem, smem,
                  pl.BlockSpec((tt,) + slab, lambda i: (jnp.minimum(i, np_tiles - 1), 0, 0)),
                  pl.BlockSpec((tt,) + slab, lambda i: (jnp.maximum(i - np_tiles, 0), 0, 0))],
        out_specs=pl.BlockSpec(memory_space=pl.ANY),
        out_shape=jax.ShapeDtypeStruct((n_tiles * tile_rows,) + slab, F32),
        scratch_shapes=[pltpu.SemaphoreType.DMA(()), pltpu.VMEM((tile_rows,) + slab, F32)],
        compiler_params=_cparams(1),
        name="moe_dispatch",
    )(pos3, pad_start, pad_count, nt, hn_p, hn_s)


def _moe_a_kernel(te_ref, nt_ref, x_ref, wg_ref, wu_ref, bg_ref, bu_ref, o_ref, *, cw):
    i = pl.program_id(0)

    @pl.when(i < nt_ref[0])
    def _():
        x = x_ref[...].reshape(x_ref.shape[0], wg_ref.shape[0]).astype(BF16)
        for c in range(wg_ref.shape[1] // cw):
            cs = pl.ds(c * cw, cw)
            gt = jnp.minimum(_dot(x, wg_ref[:, cs]) + bg_ref[:, cs], SWIGLU_LIMIT)
            up = jnp.clip(_dot(x, wu_ref[:, cs]) + bu_ref[:, cs], -SWIGLU_LIMIT, SWIGLU_LIMIT)
            hid = (up + 1.0) * (gt * jax.nn.sigmoid(SWIGLU_ALPHA * gt))
            o_ref[:, cs] = hid.astype(o_ref.dtype)

    @pl.when(i >= nt_ref[0])
    def _():
        o_ref[...] = jnp.zeros_like(o_ref)


def _moe_a(te, nt, xs, wg, wu, bg, bu, tm):
    p = xs.shape[0]
    d, dff = wg.shape[1:]
    wspec = pl.BlockSpec((None, d, dff), lambda i, te, nt: (te[i], 0, 0))
    bspec = pl.BlockSpec((None, 1, dff), lambda i, te, nt: (te[i], 0, 0))
    return pl.pallas_call(
        functools.partial(_moe_a_kernel, cw=512),
        grid_spec=pltpu.PrefetchScalarGridSpec(
            num_scalar_prefetch=2, grid=(p // tm,),
            in_specs=[_slab_spec(tm, d, lambda i, te, nt: (i, 0, 0)), wspec, wspec, bspec, bspec],
            out_specs=pl.BlockSpec((tm, dff), lambda i, te, nt: (i, 0))),
        out_shape=jax.ShapeDtypeStruct((p, dff), BF16),
        compiler_params=_cparams(1),
        name="moe_gate_up",
    )(te, nt, xs, wg, wu, bg, bu)


def _moe_b_kernel(te_ref, nt_ref, h_ref, wd_ref, bd_ref, o_ref):
    i = pl.program_id(0)

    @pl.when(i < nt_ref[0])
    def _():
        o_ref[...] = _dot(h_ref[...], wd_ref[...]) + bd_ref[...]

    @pl.when(i >= nt_ref[0])
    def _():
        o_ref[...] = jnp.zeros_like(o_ref)


def _moe_b(te, nt, hid, wd, bd, tm):
    p, dff = hid.shape
    d = wd.shape[2]
    return pl.pallas_call(
        _moe_b_kernel,
        grid_spec=pltpu.PrefetchScalarGridSpec(
            num_scalar_prefetch=2, grid=(p // tm,),
            in_specs=[pl.BlockSpec((tm, dff), lambda i, te, nt: (i, 0)),
                      pl.BlockSpec((None, dff, d), lambda i, te, nt: (te[i], 0, 0)),
                      pl.BlockSpec((None, 1, d), lambda i, te, nt: (te[i], 0, 0))],
            out_specs=pl.BlockSpec((tm, d), lambda i, te, nt: (i, 0))),
        out_shape=jax.ShapeDtypeStruct((p, d), F32),
        compiler_params=_cparams(1),
        name="moe_down",
    )(te, nt, hid, wd, bd)


def _combine_kernel(pos_ref, h_ref, w_ref, g_ref, ys_ref, y_ref, gath_ref, sem, *, tt):
    def issue(r, carry):
        for k in range(TOP_K):
            _row_copy(ys_ref, pos_ref[0, r * TOP_K + k], gath_ref.at[k], r, sem).start(priority=k % 2)
        return carry

    lax.fori_loop(0, tt, issue, 0)

    def drain(r, carry):
        for k in range(TOP_K):
            _row_copy(ys_ref, 0, gath_ref.at[k], 0, sem).wait()
        return carry

    lax.fori_loop(0, tt, drain, 0)
    w = w_ref[...]
    moe = w[:, 0:1] * gath_ref[0]
    for k in range(1, TOP_K):
        moe = moe + w[:, k:k + 1] * gath_ref[k]
    h2 = h_ref[...] + moe
    ms = jnp.mean(h2 * h2, axis=-1, keepdims=True)
    y_ref[...] = h2 * lax.rsqrt(ms + EPS) * g_ref[...]


def _combine(pos, h, topw, g_final, ys, tt):
    n, d = h.shape
    pos3 = pos.reshape(n // tt, 1, tt * TOP_K)
    return pl.pallas_call(
        functools.partial(_combine_kernel, tt=tt),
        grid=(n // tt,),
        in_specs=[pl.BlockSpec((None, 1, tt * TOP_K), lambda i: (i, 0, 0), memory_space=pltpu.SMEM),
                  pl.BlockSpec((tt, d), lambda i: (i, 0)),
                  pl.BlockSpec((tt, TOP_K), lambda i: (i, 0)),
                  _const_spec((1, d)),
                  pl.BlockSpec(memory_space=pl.ANY)],
        out_specs=pl.BlockSpec((tt, d), lambda i: (i, 0)),
        out_shape=jax.ShapeDtypeStruct((n, d), F32),
        scratch_shapes=[pltpu.VMEM((TOP_K, tt, d), F32), pltpu.SemaphoreType.DMA(())],
        compiler_params=_cparams(1),
        name="moe_combine",
    )(pos3, h, topw, g_final, ys)


def _row_tile(n, want):
    t = min(n, want)
    assert n % t == 0, (n, t)
    return t


MOE_TILE = 256


def _mixer(x3, k_past, v_past, logf_past, conv_past, W, cnt_in):
    b, t, d = x3.shape
    n = b * t
    past = 0 if k_past is None else k_past.shape[1]
    x = x3.reshape(n, d)
    transposed = t % LANES == 0
    tm = _row_tile(t, 256)
    q, k, v, kb, vb, logf, u = _in_proj(x, W["g_mix"], W["wq"], W["wk"], W["wv"], W["wf"], W["wa"], W["wg"],
                                        W["bf"], tm, t, transposed)
    logf3 = logf.reshape(b, t, N_HEADS)
    kb3 = kb.reshape(b, t, D_ATTN)
    if transposed:
        qt, vt = q, vb
        t_q = t
    else:
        t_q = -(-t // LANES) * LANES
        qt = jnp.pad(q.reshape(b, t, D_ATTN), ((0, 0), (0, t_q - t), (0, 0))).transpose(0, 2, 1)
        vt = vb.reshape(b, t, D_ATTN).transpose(0, 2, 1)
    logf_all = logf3
    if past:
        logf_all = jnp.concatenate([logf_past.astype(F32), logf3], axis=1)
        kb3 = jnp.concatenate([k_past.reshape(b, past, D_ATTN).astype(BF16), kb3], axis=1)
        vt = jnp.concatenate([v_past.reshape(b, past, D_ATTN).astype(BF16).transpose(0, 2, 1), vt], axis=2)
        hist = jnp.pad(conv_past, ((0, 0), (HIST_PAD - HIST, 0), (0, 0)))
    else:
        hist = jnp.zeros((b, HIST_PAD, D_CONV), F32)
    t_k = past + t
    if t_k % 1024 == 0 and t_q % 1024 == 0:
        tq = tk = 1024
    elif t_k % LANES == 0 and t_q % LANES == 0 and past == 0:
        tq = tk = LANES
    else:
        tq = t_q
        tk = -(-t_k // LANES) * LANES
        kb3 = jnp.pad(kb3, ((0, 0), (0, tk - t_k), (0, 0)))
        vt = jnp.pad(vt, ((0, 0), (0, 0), (0, tk - t_k)))
        logf_all = jnp.pad(logf_all, ((0, 0), (0, tk - t_k), (0, 0)))
    t_kp = kb3.shape[1]
    aug = _forget_aug(logf_all, 512 if t_kp % 512 == 0 else t_kp)
    attn = _attention(qt, kb3, aug, vt, past, tq, tk)[:, :t]
    u3 = u.reshape(b, t, D_CONV)
    c = _conv(u3, hist, W["w_dw"], W["b_dw"], W["g_ln"], W["b_ln"], _row_tile(t, 256))
    h, hn, topi, topw, rank, cnt = _out_proj(attn.reshape(n, D_ATTN), c.reshape(n, D_CONV), x, W["wo_a"], W["wo_c"],
                                             W["g_ffn"], W["wr"], W["br"], cnt_in, _row_tile(n, 256))
    new_conv = u3[:, t - HIST:] if t >= HIST else jnp.concatenate([hist[:, HIST_PAD - HIST:], u3], axis=1)[:, t:]
    outs = (k.reshape(b, t, N_HEADS, HEAD_DIM), v.reshape(b, t, N_HEADS, HEAD_DIM), logf3, new_conv)
    return (h, hn, topi, topw, rank, cnt), outs


def kernel(x_prompt, x_sample, cache_k, cache_v, cache_logf, cache_conv, g_norm_mix, w_in, b_f, w_dw, b_dw, g_ln,
           b_ln, w_out, g_norm_ffn, w_router, b_router, w_gate, b_gate, w_up, b_up, w_down, b_down, g_norm_final):
    d = x_prompt.shape[-1]
    row = lambda a: a.reshape(1, -1).astype(F32)
    wi = w_in.astype(BF16)
    s0, s1, s2, s3, s4 = D_ATTN, 2 * D_ATTN, 3 * D_ATTN, 3 * D_ATTN + N_HEADS, 3 * D_ATTN + N_HEADS + D_CONV
    wo = w_out.astype(BF16)
    W = {
        "g_mix": row(g_norm_mix),
        "wq": wi[:, :s0], "wk": wi[:, s0:s1], "wv": wi[:, s1:s2],
        "wf": jnp.pad(wi[:, s2:s3], ((0, 0), (0, LANES - N_HEADS))),
        "wa": wi[:, s3:s4], "wg": wi[:, s4:],
        "bf": jnp.pad(row(b_f), ((0, 0), (0, LANES - N_HEADS))),
        "w_dw": jnp.pad(w_dw.astype(F32), ((0, HIST_PAD - CONV_WIDTH), (0, 0))),
        "b_dw": row(b_dw), "g_ln": row(g_ln), "b_ln": row(b_ln),
        "wo_a": wo[:D_ATTN], "wo_c": wo[D_ATTN:],
        "g_ffn": row(g_norm_ffn),
        "wr": jnp.pad(w_router.astype(BF16), ((0, 0), (0, LANES - N_EXPERTS))),
        "br": jnp.pad(row(b_router), ((0, 0), (0, LANES - N_EXPERTS))),
    }
    zero_cnt = jnp.zeros((1, LANES), F32)
    (h_p, hn_p, ti_p, tw_p, rk_p, cnt_p), outs_p = _mixer(x_prompt, None, None, None, None, W, zero_cnt)
    (h_s, hn_s, ti_s, tw_s, rk_s, cnt), outs_s = _mixer(x_sample, cache_k, cache_v, cache_logf, cache_conv, W, cnt_p)

    n_p, n_s = h_p.shape[0], h_s.shape[0]
    tmm = MOE_TILE
    n_tiles = (n_p + n_s) * TOP_K // tmm + N_EXPERTS
    counts = cnt[0, :N_EXPERTS].astype(I32)
    padded = (counts + tmm - 1) // tmm * tmm
    ends = jnp.cumsum(padded)
    offsets = ends - padded
    pos_p = offsets[ti_p] + rk_p
    pos_s = offsets[ti_s] + rk_s
    nt = (ends[-1] // tmm).reshape(1).astype(I32)
    tile_start = jnp.arange(n_tiles, dtype=I32) * tmm
    te = jnp.minimum(jnp.sum((ends[None, :] <= tile_start[:, None]).astype(I32), axis=1), N_EXPERTS - 1)
    pad_start = offsets + counts
    pad_count = padded - counts

    tt = _row_tile(n_s, _row_tile(n_p, 256))
    xs = _dispatch(jnp.concatenate([pos_p, pos_s], axis=0), hn_p, hn_s, pad_start, pad_count, nt, n_tiles, tmm, tt)
    hid = _moe_a(te, nt, xs, w_gate.astype(BF16), w_up.astype(BF16), b_gate[:, None, :], b_up[:, None, :], tmm)
    ys = _moe_b(te, nt, hid, w_down.astype(BF16), b_down[:, None, :], tmm)
    g_fin = row(g_norm_final)
    y_p = _combine(pos_p, h_p, tw_p, g_fin, ys, _row_tile(n_p, 256))
    y_s = _combine(pos_s, h_s, tw_s, g_fin, ys, _row_tile(n_s, 256))
    return (y_p.reshape(x_prompt.shape), y_s.reshape(x_sample.shape)) + outs_p + outs_s
```
